```python
import jax, jax.numpy as jnp
from jax import lax
import numpy as np

D_MODEL = 1024
BATCH = 4
SEQ = 8192
DEPTH = 2
DEC_BATCH = 8
DEC_SEQ = 64
PAST_LEN = 4096

CHUNK = 64
N_MIXERS = 2
N_CONV_LAYERS = (DEPTH + 1) // 2
N_MLA_LAYERS = DEPTH // 2
CONV_WIDTH = 3
CONV_DIM = D_MODEL
N_HEADS = 16
QK_NOPE_DIM = 64
QK_ROPE_DIM = 32
V_HEAD_DIM = 64
QK_DIM = QK_NOPE_DIM + QK_ROPE_DIM
Q_LORA_RANK = 256
KV_LORA_RANK = 128
MLA_GATE_DIM = N_HEADS * V_HEAD_DIM
MLA_IN_DIM = Q_LORA_RANK + KV_LORA_RANK + QK_ROPE_DIM + MLA_GATE_DIM
ROPE_THETA = 10000.0
Q_BLOCK = 128
EPS = 1e-6
SM_SCALE = QK_DIM ** -0.5
NEG_INF = -1e30

kernel_name = "hybrid_shortconv_mla_stream_step"


def rms_norm(x, g):
    xf = x.astype(jnp.float32)
    y = xf * lax.rsqrt(jnp.mean(xf * xf, axis=-1, keepdims=True) + EPS)
    return (y * g.astype(jnp.float32)).astype(x.dtype)


def rope_cos_sin(pos):
    inv = 1.0 / (ROPE_THETA ** (jnp.arange(0, QK_ROPE_DIM, 2, dtype=jnp.float32) / QK_ROPE_DIM))
    ang = pos.astype(jnp.float32)[:, None] * inv[None, :]
    return jnp.cos(ang), jnp.sin(ang)


def apply_rope(x, cos, sin):
    xf = x.astype(jnp.float32)
    x1, x2 = xf[..., :QK_ROPE_DIM // 2], xf[..., QK_ROPE_DIM // 2:]
    out = jnp.concatenate([x1 * cos - x2 * sin, x2 * cos + x1 * sin], axis=-1)
    return out.astype(x.dtype)


def short_conv_mixer(h, conv_prev, w_in, conv_w, w_out):
    T = h.shape[1]
    b_gate, c_gate, xv, z = jnp.split(h @ w_in, 4, axis=-1)
    u = c_gate * xv
    up = jnp.concatenate([conv_prev.astype(u.dtype), u], axis=1)
    conv = up[:, 0:T] * conv_w[0]
    for k in range(1, CONV_WIDTH):
        conv = conv + up[:, k:k + T] * conv_w[k]
    y = (jax.nn.silu(z) * b_gate * conv) @ w_out
    return y, up[:, -(CONV_WIDTH - 1):]


def mla_project(h, pos, w_in, q_norm_g, w_qb, kv_norm_g):
    B, T, _ = h.shape
    p = h @ w_in
    q_lat = p[..., :Q_LORA_RANK]
    kv_lat = p[..., Q_LORA_RANK:Q_LORA_RANK + KV_LORA_RANK]
    k_rope = p[..., Q_LORA_RANK + KV_LORA_RANK:Q_LORA_RANK + KV_LORA_RANK + QK_ROPE_DIM]
    z = p[..., Q_LORA_RANK + KV_LORA_RANK + QK_ROPE_DIM:]
    q = (rms_norm(q_lat, q_norm_g) @ w_qb).reshape(B, T, N_HEADS, QK_DIM)
    cos, sin = rope_cos_sin(pos)
    q_rope = apply_rope(q[..., QK_NOPE_DIM:], cos[:, None, :], sin[:, None, :])
    q = jnp.concatenate([q[..., :QK_NOPE_DIM], q_rope], axis=-1)
    k_rope = apply_rope(k_rope, cos, sin)
    c_kv = rms_norm(kv_lat, kv_norm_g)
    return q, c_kv, k_rope, z


def mla_expand_kv(c_kv, k_rope, w_kvb):
    B, S, _ = c_kv.shape
    kv = (c_kv @ w_kvb).reshape(B, S, N_HEADS, QK_NOPE_DIM + V_HEAD_DIM)
    k_pe = jnp.broadcast_to(k_rope[:, :, None, :].astype(kv.dtype), (B, S, N_HEADS, QK_ROPE_DIM))
    k = jnp.concatenate([kv[..., :QK_NOPE_DIM], k_pe], axis=-1)
    v = kv[..., QK_NOPE_DIM:]
    return k, v


def attend(q, k, v, mask):
    s = jnp.einsum('bqhd,bkhd->bhqk', q, k).astype(jnp.float32) * SM_SCALE
    if mask is not None:
        s = jnp.where(mask[None, None], s, NEG_INF)
    p = jax.nn.softmax(s, axis=-1).astype(v.dtype)
    return jnp.einsum('bhqk,bkhd->bqhd', p, v)


def mla_prompt_attention(q, k, v):
    B, S = q.shape[0], q.shape[1]
    k_chunk = jnp.arange(S) // CHUNK

    def block(i):
        start = i * Q_BLOCK
        qb = lax.dynamic_slice_in_dim(q, start, Q_BLOCK, axis=1)
        q_chunk = (start + jnp.arange(Q_BLOCK)) // CHUNK
        mask = k_chunk[None, :] <= q_chunk[:, None]
        return attend(qb, k, v, mask)

    o = lax.map(block, jnp.arange(S // Q_BLOCK))
    return jnp.moveaxis(o, 0, 1).reshape(B, S, N_HEADS * V_HEAD_DIM)


def setup_inputs(seed: int = 0) -> dict:
    key = jax.random.key(seed)
    ks = jax.random.split(key, 20)
    f32 = jnp.float32

    def nrm(k, shape, scale):
        return jax.random.normal(k, shape, f32) * scale

    return {
        "x_prompt": nrm(ks[0], (BATCH, SEQ, D_MODEL), 1.0),
        "x_sample": nrm(ks[1], (DEC_BATCH, DEC_SEQ, D_MODEL), 1.0),
        "state_conv": nrm(ks[2], (N_CONV_LAYERS, DEC_BATCH, CONV_WIDTH - 1, CONV_DIM), 1.0),
        "cache_ckv": nrm(ks[3], (N_MLA_LAYERS, DEC_BATCH, PAST_LEN, KV_LORA_RANK), 1.0),
        "cache_krope": nrm(ks[4], (N_MLA_LAYERS, DEC_BATCH, PAST_LEN, QK_ROPE_DIM), 1.0),
        "norm_g": 1.0 + nrm(ks[5], (DEPTH, D_MODEL), 0.05),
        "final_norm_g": 1.0 + nrm(ks[6], (D_MODEL,), 0.05),
        "conv_w_in": nrm(ks[7], (N_CONV_LAYERS, D_MODEL, 4 * CONV_DIM), D_MODEL ** -0.5),
        "conv_w": nrm(ks[8], (N_CONV_LAYERS, CONV_WIDTH, CONV_DIM), CONV_WIDTH ** -0.5),
        "conv_w_out": nrm(ks[9], (N_CONV_LAYERS, CONV_DIM, D_MODEL), CONV_DIM ** -0.5),
        "mla_w_in": nrm(ks[10], (N_MLA_LAYERS, D_MODEL, MLA_IN_DIM), D_MODEL ** -0.5),
        "mla_q_norm_g": 1.0 + nrm(ks[11], (N_MLA_LAYERS, Q_LORA_RANK), 0.05),
        "mla_w_qb": nrm(ks[12], (N_MLA_LAYERS, Q_LORA_RANK, N_HEADS * QK_DIM), Q_LORA_RANK ** -0.5),
        "mla_kv_norm_g": 1.0 + nrm(ks[13], (N_MLA_LAYERS, KV_LORA_RANK), 0.05),
        "mla_w_kvb": nrm(ks[14], (N_MLA_LAYERS, KV_LORA_RANK, N_HEADS * (QK_NOPE_DIM + V_HEAD_DIM)), KV_LORA_RANK ** -0.5),
        "mla_w_out": nrm(ks[15], (N_MLA_LAYERS, MLA_GATE_DIM, D_MODEL), MLA_GATE_DIM ** -0.5),
    }


def reference(x_prompt, x_sample, state_conv, cache_ckv, cache_krope, norm_g, final_norm_g,
              conv_w_in, conv_w, conv_w_out, mla_w_in, mla_q_norm_g, mla_w_qb,
              mla_kv_norm_g, mla_w_kvb, mla_w_out):
    xp, xs = x_prompt, x_sample
    Bp, S, _ = xp.shape
    Bs, T, _ = xs.shape
    pos_p = jnp.arange(S, dtype=jnp.int32)
    pos_s = PAST_LEN + jnp.arange(T, dtype=jnp.int32)
    conv_p_states, conv_s_states = [], []
    ckv_p_rows, krope_p_rows, ckv_s_rows, krope_s_rows = [], [], [], []

    for i in range(DEPTH):
        j = i // N_MIXERS
        hp = rms_norm(xp, norm_g[i])
        hs = rms_norm(xs, norm_g[i])
        if i % N_MIXERS == 0:
            zeros_prev = jnp.zeros((Bp, CONV_WIDTH - 1, CONV_DIM), xp.dtype)
            yp, st_p = short_conv_mixer(hp, zeros_prev, conv_w_in[j], conv_w[j], conv_w_out[j])
            ys, st_s = short_conv_mixer(hs, state_conv[j], conv_w_in[j], conv_w[j], conv_w_out[j])
            conv_p_states.append(st_p)
            conv_s_states.append(st_s)
        else:
            qp, ckv_p, kr_p, zp = mla_project(hp, pos_p, mla_w_in[j], mla_q_norm_g[j], mla_w_qb[j], mla_kv_norm_g[j])
            kp, vp = mla_expand_kv(ckv_p, kr_p, mla_w_kvb[j])
            op = mla_prompt_attention(qp, kp, vp)
            yp = (jax.nn.silu(zp) * op) @ mla_w_out[j]

            qs, ckv_s, kr_s, zs = mla_project(hs, pos_s, mla_w_in[j], mla_q_norm_g[j], mla_w_qb[j], mla_kv_norm_g[j])
            ckv_all = jnp.concatenate([cache_ckv[j].astype(ckv_s.dtype), ckv_s], axis=1)
            kr_all = jnp.concatenate([cache_krope[j].astype(kr_s.dtype), kr_s], axis=1)
            ks_, vs_ = mla_expand_kv(ckv_all, kr_all, mla_w_kvb[j])
            os_ = attend(qs, ks_, vs_, None).reshape(Bs, T, N_HEADS * V_HEAD_DIM)
            ys = (jax.nn.silu(zs) * os_) @ mla_w_out[j]

            ckv_p_rows.append(ckv_p)
            krope_p_rows.append(kr_p)
            ckv_s_rows.append(ckv_s)
            krope_s_rows.append(kr_s)
        xp = xp + yp
        xs = xs + ys

    y_prompt = rms_norm(xp, final_norm_g)
    y_sample = rms_norm(xs, final_norm_g)
    new_conv_prompt = jnp.stack(conv_p_states, axis=0)
    new_ckv_prompt = jnp.stack(ckv_p_rows, axis=0)
    new_krope_prompt = jnp.stack(krope_p_rows, axis=0)
    new_conv_sample = jnp.stack(conv_s_states, axis=0)
    new_ckv_sample = jnp.stack(ckv_s_rows, axis=0)
    new_krope_sample = jnp.stack(krope_s_rows, axis=0)
    return (y_prompt, y_sample, new_conv_prompt, new_ckv_prompt, new_krope_prompt, new_conv_sample, new_ckv_sample, new_krope_sample)
```

```python
import functools
import math

import jax
import jax.numpy as jnp
from jax import lax
from jax.experimental import pallas as pl
from jax.experimental.pallas import tpu as pltpu

D_MODEL = 1024
CHUNK = 64
CONV_WIDTH = 3
N_HEADS = 16
QK_NOPE_DIM = 64
QK_ROPE_DIM = 32
V_HEAD_DIM = 64
QK_DIM = QK_NOPE_DIM + QK_ROPE_DIM
Q_LORA_RANK = 256
KV_LORA_RANK = 128
MLA_GATE_DIM = N_HEADS * V_HEAD_DIM
ROPE_THETA = 10000.0
EPS = 1e-6
NEG_INF = -1e30
PAST_LEN = 4096

LANES = 128
HEAD_BLOCK = LANES
N_PAIRS = N_HEADS // 2
ROPE_LO = QK_NOPE_DIM
ROPE_MID = ROPE_LO + QK_ROPE_DIM // 2
ROPE_HI = ROPE_LO + QK_ROPE_DIM
Q_SCALE = (QK_DIM ** -0.5) * math.log2(math.e)

VMEM_LIMIT = 56 * 1024 * 1024
F32 = jnp.float32
BF16 = jnp.bfloat16


def _rms(x, g):
    ms = jnp.mean(x * x, axis=-1, keepdims=True)
    return x * lax.rsqrt(ms + EPS) * g


def _const_spec(shape):
    nd = len(shape)
    return pl.BlockSpec(shape, lambda *_: (0,) * nd)


def _conv_kernel(x_ref, prev_ref, g_ref, win_ref, cw_ref, wout_ref, xo_ref, st_ref, ubuf):
    j = pl.program_id(1)
    t = x_ref.shape[1]
    halo = CONV_WIDTH - 1
    base = 8

    @pl.when(j == 0)
    def _():
        ubuf[base - halo:base, :] = prev_ref[0]

    x = x_ref[0]
    hb = _rms(x, g_ref[...]).astype(BF16)

    def proj(k):
        return jnp.dot(hb, win_ref[:, k * D_MODEL:(k + 1) * D_MODEL],
                       preferred_element_type=F32)

    u = proj(1) * proj(2)
    ubuf[base:base + t, :] = u
    conv = (ubuf[base - 2:base - 2 + t, :] * cw_ref[0:1, :]
            + ubuf[base - 1:base - 1 + t, :] * cw_ref[1:2, :]
            + u * cw_ref[2:3, :])
    new_halo = ubuf[base + t - halo:base + t, :]
    ubuf[base - halo:base, :] = new_halo
    z = proj(3)
    gated = (z * jax.nn.sigmoid(z)) * proj(0) * conv
    y = jnp.dot(gated.astype(BF16), wout_ref[...], preferred_element_type=F32)
    xo_ref[0] = x + y

    @pl.when(j == pl.num_programs(1) - 1)
    def _():
        st_ref[0] = new_halo


def _conv_layer(x, prev, g, w_in, conv_w, w_out, tile):
    b, s, d = x.shape
    n = s // tile
    return pl.pallas_call(
        _conv_kernel,
        grid=(b, n),
        in_specs=[
            pl.BlockSpec((1, tile, d), lambda i, j: (i, j, 0)),
            pl.BlockSpec((1, CONV_WIDTH - 1, d), lambda i, j: (i, 0, 0)),
            _const_spec((1, d)),
            _const_spec(w_in.shape),
            _const_spec(conv_w.shape),
            _const_spec(w_out.shape),
        ],
        out_specs=[
            pl.BlockSpec((1, tile, d), lambda i, j: (i, j, 0)),
            pl.BlockSpec((1, CONV_WIDTH - 1, d), lambda i, j: (i, 0, 0)),
        ],
        out_shape=[
            jax.ShapeDtypeStruct((b, s, d), F32),
            jax.ShapeDtypeStruct((b, CONV_WIDTH - 1, d), F32),
        ],
        scratch_shapes=[pltpu.VMEM((tile + 8, d), F32)],
        compiler_params=pltpu.CompilerParams(
            dimension_semantics=("arbitrary", "arbitrary"),
            vmem_limit_bytes=VMEM_LIMIT),
        name="conv_layer",
    )(x, prev, g, w_in, conv_w, w_out)


def _rope_table_kernel(inv_ref, cos_ref, sin_ref, *, offset):
    t = cos_ref.shape[0]
    row = lax.broadcasted_iota(jnp.int32, (t, LANES), 0)
    lane = lax.broadcasted_iota(jnp.int32, (t, LANES), 1)
    pos = (offset + pl.program_id(0) * t + row).astype(F32)
    ang = pos * inv_ref[...]
    cos_ref[...] = jnp.cos(ang)
    s = jnp.sin(ang)
    sin_ref[...] = jnp.where(lane < ROPE_MID, -s, s)


def _rope_tables(inv_lane, s, offset, tile):
    return pl.pallas_call(
        functools.partial(_rope_table_kernel, offset=offset),
        grid=(s // tile,),
        in_specs=[_const_spec((1, LANES))],
        out_specs=[pl.BlockSpec((tile, LANES), lambda i: (i, 0))] * 2,
        out_shape=[jax.ShapeDtypeStruct((s, LANES), F32)] * 2,
        compiler_params=pltpu.CompilerParams(dimension_semantics=("arbitrary",)),
        name="rope_tables",
    )(inv_lane)


def _rope(v, cos, sin_signed, lane):
    half = QK_ROPE_DIM // 2
    partner = jnp.where(lane < ROPE_MID,
                        pltpu.roll(v, LANES - half, 1),
                        pltpu.roll(v, half, 1))
    return v * cos + partner * sin_signed


def _expand_kv(ckv, kr_block, wk_ref, wv_ref, k_ref, v_ref):
    ckv_b = ckv.astype(BF16)
    kn = jnp.dot(ckv_b, wk_ref[...], preferred_element_type=F32)
    vv = jnp.dot(ckv_b, wv_ref[...], preferred_element_type=F32)
    for h in range(N_HEADS):
        k_ref[0, h] = (kn[:, h * HEAD_BLOCK:(h + 1) * HEAD_BLOCK] + kr_block).astype(BF16)
    for p in range(N_PAIRS):
        v_ref[0, p] = vv[:, p * LANES:(p + 1) * LANES].astype(BF16)


def _mla_project_kernel(x_ref, g_ref, win_ref, qg_ref, wqb_ref, kvg_ref, wk_ref, wv_ref,
                        cos_ref, sin_ref, q_ref, zg_ref, ckv_ref, kr_ref, *kv_refs):
    t = x_ref.shape[1]
    hb = _rms(x_ref[0], g_ref[...]).astype(BF16)
    o_kv = Q_LORA_RANK
    o_kr = o_kv + KV_LORA_RANK
    o_z = o_kr + HEAD_BLOCK

    def proj(lo, hi):
        return jnp.dot(hb, win_ref[:, lo:hi], preferred_element_type=F32)

    z = proj(o_z, o_z + MLA_GATE_DIM)
    zg_ref[0] = (z * jax.nn.sigmoid(z)).astype(BF16)

    ckv = _rms(proj(o_kv, o_kr), kvg_ref[...])
    ckv_ref[0] = ckv

    cos = cos_ref[...]
    sin_signed = sin_ref[...]
    lane = lax.broadcasted_iota(jnp.int32, (t, LANES), 1)
    kr_block = _rope(proj(o_kr, o_z), cos, sin_signed, lane)
    kr_ref[0] = kr_block[:, ROPE_LO:ROPE_HI]

    qn = _rms(proj(0, o_kv), qg_ref[...]).astype(BF16)
    q = jnp.dot(qn, wqb_ref[...], preferred_element_type=F32)
    for h in range(N_HEADS):
        qh = _rope(q[:, h * HEAD_BLOCK:(h + 1) * HEAD_BLOCK], cos, sin_signed, lane)
        q_ref[0, h] = (qh * Q_SCALE).astype(BF16)

    if kv_refs:
        _expand_kv(ckv, kr_block, wk_ref, wv_ref, *kv_refs)


def _mla_project(x, g, w_in2, qg, wqb, kvg, wk, wv, cos, sin_signed, tile, emit_kv):
    b, s, d = x.shape
    n = s // tile
    tok = lambda w: pl.BlockSpec((1, tile, w), lambda i, j: (i, j, 0))
    head = lambda nh: pl.BlockSpec((1, nh, tile, LANES), lambda i, j: (i, 0, j, 0))
    out_specs = [head(N_HEADS), tok(MLA_GATE_DIM), tok(KV_LORA_RANK), tok(QK_ROPE_DIM)]
    out_shape = [
        jax.ShapeDtypeStruct((b, N_HEADS, s, HEAD_BLOCK), BF16),
        jax.ShapeDtypeStruct((b, s, MLA_GATE_DIM), BF16),
        jax.ShapeDtypeStruct((b, s, KV_LORA_RANK), F32),
        jax.ShapeDtypeStruct((b, s, QK_ROPE_DIM), F32),
    ]
    if emit_kv:
        out_specs += [head(N_HEADS), head(N_PAIRS)]
        out_shape += [jax.ShapeDtypeStruct((b, N_HEADS, s, HEAD_BLOCK), BF16),
                      jax.ShapeDtypeStruct((b, N_PAIRS, s, LANES), BF16)]
    return pl.pallas_call(
        _mla_project_kernel,
        grid=(b, n),
        in_specs=[
            tok(d),
            _const_spec((1, d)),
            _const_spec(w_in2.shape),
            _const_spec(qg.shape),
            _const_spec(wqb.shape),
            _const_spec(kvg.shape),
            _const_spec(wk.shape),
            _const_spec(wv.shape),
            pl.BlockSpec((tile, LANES), lambda i, j: (j, 0)),
            pl.BlockSpec((tile, LANES), lambda i, j: (j, 0)),
        ],
        out_specs=out_specs,
        out_shape=out_shape,
        compiler_params=pltpu.CompilerParams(
            dimension_semantics=("arbitrary", "arbitrary"),
            vmem_limit_bytes=VMEM_LIMIT),
        name="mla_project",
    )(x, g, w_in2, qg, wqb, kvg, wk, wv, cos, sin_signed)


def _mla_expand_kernel(ckv_ref, kr_ref, place_ref, wk_ref, wv_ref, k_ref, v_ref):
    kr_block = jnp.dot(kr_ref[0].astype(BF16), place_ref[...], preferred_element_type=F32)
    _expand_kv(ckv_ref[0], kr_block, wk_ref, wv_ref, k_ref, v_ref)


def _mla_expand(ckv, kr, place, wk, wv, tile):
    b, s, _ = ckv.shape
    n = s // tile
    tok = lambda w: pl.BlockSpec((1, tile, w), lambda i, j: (i, j, 0))
    head = lambda nh: pl.BlockSpec((1, nh, tile, LANES), lambda i, j: (i, 0, j, 0))
    return pl.pallas_call(
        _mla_expand_kernel,
        grid=(b, n),
        in_specs=[tok(KV_LORA_RANK), tok(QK_ROPE_DIM), _const_spec(place.shape),
                  _const_spec(wk.shape), _const_spec(wv.shape)],
        out_specs=[head(N_HEADS), head(N_PAIRS)],
        out_shape=[jax.ShapeDtypeStruct((b, N_HEADS, s, HEAD_BLOCK), BF16),
                   jax.ShapeDtypeStruct((b, N_PAIRS, s, LANES), BF16)],
        compiler_params=pltpu.CompilerParams(
            dimension_semantics=("arbitrary", "arbitrary"),
            vmem_limit_bytes=VMEM_LIMIT),
        name="mla_expand",
    )(ckv, kr, place, wk, wv)


def _attn_kernel(q_ref, k_ref, v_ref, o_ref, m_sc, l_sc, acc_sc, *, tk, causal, kv_len):
    tq = q_ref.shape[2]
    n_group = q_ref.shape[1]
    if causal:
        n_full = pl.program_id(2)
    else:
        n_full = kv_len // tk
    lane_q = lax.broadcasted_iota(jnp.int32, (tq, LANES), 1)
    lane_k = lax.broadcasted_iota(jnp.int32, (tk, LANES), 1)
    lo_q = lane_q < V_HEAD_DIM
    lo_k = lane_k < V_HEAD_DIM

    def last_tile_mask():
        row = lax.broadcasted_iota(jnp.int32, (tq, tk), 0)
        col = lax.broadcasted_iota(jnp.int32, (tq, tk), 1)
        if causal:
            return (col // CHUNK) <= (row // CHUNK)
        return col < (kv_len - (kv_len // tk) * tk)

    for pr in range(n_group // 2):
        m_sc[...] = jnp.full(m_sc.shape, NEG_INF, F32)
        l_sc[...] = jnp.zeros(l_sc.shape, F32)
        acc_sc[...] = jnp.zeros(acc_sc.shape, F32)

        def tile(j, mask):
            koff = pl.multiple_of(j * tk, tk)
            ps = []
            alphas = []
            for hh in range(2):
                h = 2 * pr + hh
                q = q_ref[0, h]
                k = k_ref[0, h, pl.ds(koff, tk), :]
                s = lax.dot_general(q, k, (((1,), (1,)), ((), ())),
                                    preferred_element_type=F32)
                if mask is not None:
                    s = jnp.where(mask, s, NEG_INF)
                m_prev = m_sc[hh]
                m_new = jnp.maximum(m_prev, jnp.max(s, axis=1, keepdims=True))
                alpha = jnp.exp2(m_prev - m_new)
                p = jnp.exp2(s - m_new[:, 0:1])
                l_sc[hh] = alpha * l_sc[hh] + jnp.sum(p, axis=1, keepdims=True)
                m_sc[hh] = m_new
                ps.append(p.astype(BF16))
                alphas.append(alpha)
            v = v_ref[0, pr, pl.ds(koff, tk), :]
            zero = jnp.zeros_like(v)
            pv = (jnp.dot(ps[0], jnp.where(lo_k, v, zero), preferred_element_type=F32)
                  + jnp.dot(ps[1], jnp.where(lo_k, zero, v), preferred_element_type=F32))
            acc_sc[...] = acc_sc[...] * jnp.where(lo_q, alphas[0], alphas[1]) + pv

        def body(j, carry):
            tile(j, None)
            return carry

        lax.fori_loop(0, n_full, body, 0)
        tile(n_full, last_tile_mask())
        inv_l = jnp.where(lo_q, 1.0 / l_sc[0], 1.0 / l_sc[1])
        o_ref[0, pr] = (acc_sc[...] * inv_l).astype(o_ref.dtype)


def _attention(q, k, v, tq, tk, group, causal, kv_len):
    b, nh, sq, _ = q.shape
    skv = k.shape[2]
    n_q = sq // tq
    gp = group // 2
    return pl.pallas_call(
        functools.partial(_attn_kernel, tk=tk, causal=causal, kv_len=kv_len),
        grid=(b, nh // group, n_q),
        in_specs=[
            pl.BlockSpec((1, group, tq, HEAD_BLOCK), lambda i, g, j: (i, g, j, 0)),
            pl.BlockSpec((1, group, skv, HEAD_BLOCK), lambda i, g, j: (i, g, 0, 0)),
            pl.BlockSpec((1, gp, skv, LANES), lambda i, g, j: (i, g, 0, 0)),
        ],
        out_specs=pl.BlockSpec((1, gp, tq, LANES), lambda i, g, j: (i, g, j, 0)),
        out_shape=jax.ShapeDtypeStruct((b, nh // 2, sq, LANES), BF16),
        scratch_shapes=[pltpu.VMEM((2, tq, LANES), F32),
                        pltpu.VMEM((2, tq, LANES), F32),
                        pltpu.VMEM((tq, LANES), F32)],
        compiler_params=pltpu.CompilerParams(
            dimension_semantics=("arbitrary", "arbitrary", "arbitrary"),
            vmem_limit_bytes=VMEM_LIMIT),
        name="attention",
    )(q, k, v)


def _out_kernel(o_ref, zg_ref, x_ref, w_ref, g_ref, y_ref):
    o = jnp.concatenate([o_ref[0, p] for p in range(N_PAIRS)], axis=1)
    gated = (zg_ref[0].astype(F32) * o.astype(F32)).astype(BF16)
    x2 = x_ref[0] + jnp.dot(gated, w_ref[...], preferred_element_type=F32)
    y_ref[0] = _rms(x2, g_ref[...])


def _out_layer(o, zg, x, w_out, g, tile):
    b, s, d = x.shape
    n = s // tile
    tok = lambda w: pl.BlockSpec((1, tile, w), lambda i, j: (i, j, 0))
    return pl.pallas_call(
        _out_kernel,
        grid=(b, n),
        in_specs=[pl.BlockSpec((1, N_PAIRS, tile, LANES), lambda i, j: (i, 0, j, 0)),
                  tok(MLA_GATE_DIM), tok(d), _const_spec(w_out.shape), _const_spec((1, d))],
        out_specs=tok(d),
        out_shape=jax.ShapeDtypeStruct((b, s, d), F32),
        compiler_params=pltpu.CompilerParams(
            dimension_semantics=("arbitrary", "arbitrary"),
            vmem_limit_bytes=VMEM_LIMIT),
        name="out_layer",
    )(o, zg, x, w_out, g)


def _prep_weights(mla_w_in, mla_w_qb, mla_w_kvb):
    o_kv = Q_LORA_RANK
    o_kr = o_kv + KV_LORA_RANK
    o_z = o_kr + QK_ROPE_DIM
    d = mla_w_in.shape[0]
    kr_cols = jnp.concatenate(
        [jnp.zeros((d, ROPE_LO), F32), mla_w_in[:, o_kr:o_z],
         jnp.zeros((d, HEAD_BLOCK - ROPE_HI), F32)], axis=1)
    w_in2 = jnp.concatenate([mla_w_in[:, :o_kr], kr_cols, mla_w_in[:, o_z:]], axis=1)

    wqb = mla_w_qb.reshape(Q_LORA_RANK, N_HEADS, QK_DIM)
    wqb = jnp.pad(wqb, ((0, 0), (0, 0), (0, HEAD_BLOCK - QK_DIM)))
    wqb = wqb.reshape(Q_LORA_RANK, N_HEADS * HEAD_BLOCK)

    wkvb = mla_w_kvb.reshape(KV_LORA_RANK, N_HEADS, QK_NOPE_DIM + V_HEAD_DIM)
    wk = jnp.pad(wkvb[:, :, :QK_NOPE_DIM], ((0, 0), (0, 0), (0, HEAD_BLOCK - QK_NOPE_DIM)))
    wk = wk.reshape(KV_LORA_RANK, N_HEADS * HEAD_BLOCK)
    wv = wkvb[:, :, QK_NOPE_DIM:].reshape(KV_LORA_RANK, N_HEADS * V_HEAD_DIM)
    return w_in2.astype(BF16), wqb.astype(BF16), wk.astype(BF16), wv.astype(BF16)


def _inv_freq_lanes():
    inv = 1.0 / (ROPE_THETA ** (jnp.arange(0, QK_ROPE_DIM, 2, dtype=F32) / QK_ROPE_DIM))
    z = lambda n: jnp.zeros((n,), F32)
    return jnp.concatenate([z(ROPE_LO), inv, inv, z(HEAD_BLOCK - ROPE_HI)])[None, :]


def _rope_placement():
    rows = jnp.arange(QK_ROPE_DIM)
    return jnp.zeros((QK_ROPE_DIM, HEAD_BLOCK), F32).at[rows, ROPE_LO + rows].set(1.0).astype(BF16)


def kernel(x_prompt, x_sample, state_conv, cache_ckv, cache_krope, norm_g, final_norm_g,
           conv_w_in, conv_w, conv_w_out, mla_w_in, mla_q_norm_g, mla_w_qb,
           mla_kv_norm_g, mla_w_kvb, mla_w_out):
    bp, s, d = x_prompt.shape
    bs, t, _ = x_sample.shape
    past = cache_ckv.shape[2]

    tok_tile = 256
    attn_tile = 512
    group = 4

    g0 = norm_g[0][None, :]
    g1 = norm_g[1][None, :]
    gf = final_norm_g[None, :]
    cw_in = conv_w_in[0].astype(BF16)
    cw_out = conv_w_out[0].astype(BF16)
    w_in2, wqb, wk, wv = _prep_weights(mla_w_in[0], mla_w_qb[0], mla_w_kvb[0])
    qg = mla_q_norm_g[0][None, :]
    kvg = mla_kv_norm_g[0][None, :]
    w_out = mla_w_out[0].astype(BF16)
    inv_lane = _inv_freq_lanes()

    zeros_prev = jnp.zeros((bp, CONV_WIDTH - 1, d), F32)
    xp1, st_p = _conv_layer(x_prompt, zeros_prev, g0, cw_in, conv_w[0], cw_out, tok_tile)
    xs1, st_s = _conv_layer(x_sample, state_conv[0], g0, cw_in, conv_w[0], cw_out, t)

    cos_p, sin_p = _rope_tables(inv_lane, s, 0, 512)
    cos_s, sin_s = _rope_tables(inv_lane, t, past, t)

    qp, zg_p, ckv_p, kr_p, kp, vp = _mla_project(
        xp1, g1, w_in2, qg, wqb, kvg, wk, wv, cos_p, sin_p, tok_tile, True)
    qs, zg_s, ckv_s, kr_s = _mla_project(
        xs1, g1, w_in2, qg, wqb, kvg, wk, wv, cos_s, sin_s, t, False)

    op = _attention(qp, kp, vp, attn_tile, attn_tile, group, True, s)

    kv_len = past + t
    kv_pad = -(-kv_len // attn_tile) * attn_tile
    pad = kv_pad - kv_len
    ckv_all = jnp.concatenate(
        [cache_ckv[0], ckv_s, jnp.zeros((bs, pad, KV_LORA_RANK), F32)], axis=1)
    kr_all = jnp.concatenate(
        [cache_krope[0], kr_s, jnp.zeros((bs, pad, QK_ROPE_DIM), F32)], axis=1)
    ks, vs = _mla_expand(ckv_all, kr_all, _rope_placement(), wk, wv, attn_tile)
    os_ = _attention(qs, ks, vs, t, attn_tile, group, False, kv_len)

    y_prompt = _out_layer(op, zg_p, xp1, w_out, gf, tok_tile)
    y_sample = _out_layer(os_, zg_s, xs1, w_out, gf, t)

    return (y_prompt, y_sample, st_p[None], ckv_p[None], kr_p[None],
            st_s[None], ckv_s[None], kr_s[None])
```

```python
import functools
import math

import jax
import jax.numpy as jnp
from jax import lax
from jax.experimental import pallas as pl
from jax.experimental.pallas import tpu as pltpu

D_MODEL = 1024
CHUNK = 64
CONV_WIDTH = 3
N_HEADS = 16
QK_NOPE_DIM = 64
QK_ROPE_DIM = 32
V_HEAD_DIM = 64
QK_DIM = QK_NOPE_DIM + QK_ROPE_DIM
Q_LORA_RANK = 256
KV_LORA_RANK = 128
MLA_GATE_DIM = N_HEADS * V_HEAD_DIM
ROPE_THETA = 10000.0
EPS = 1e-6
NEG_INF = -1e30
PAST_LEN = 4096

LANES = 128
HEAD_BLOCK = LANES
N_PAIRS = N_HEADS // 2
VT_ROWS = V_HEAD_DIM + 16
ROPE_LO = QK_NOPE_DIM
ROPE_MID = ROPE_LO + QK_ROPE_DIM // 2
ROPE_HI = ROPE_LO + QK_ROPE_DIM
Q_SCALE = (QK_DIM ** -0.5) * math.log2(math.e)

SCORES_AHEAD = 3
VMEM_LIMIT =56 * 1024 * 1024
F32 = jnp.float32
BF16 = jnp.bfloat16


def _rms(x, g):
    ms = jnp.mean(x * x, axis=-1, keepdims=True)
    return x * lax.rsqrt(ms + EPS) * g


def _const_spec(shape):
    nd = len(shape)
    return pl.BlockSpec(shape, lambda *_: (0,) * nd)


def _conv_kernel(x_ref, prev_ref, g_ref, win_ref, cw_ref, wout_ref, xo_ref, st_ref, ubuf):
    j = pl.program_id(1)
    t = x_ref.shape[1]
    halo = CONV_WIDTH - 1
    base = 8

    @pl.when(j == 0)
    def _():
        ubuf[base - halo:base, :] = prev_ref[0]

    x = x_ref[0]
    hb = _rms(x, g_ref[...]).astype(BF16)

    def proj(k):
        return jnp.dot(hb, win_ref[:, k * D_MODEL:(k + 1) * D_MODEL],
                       preferred_element_type=F32)

    u = proj(1) * proj(2)
    ubuf[base:base + t, :] = u
    conv = (ubuf[base - 2:base - 2 + t, :] * cw_ref[0:1, :]
            + ubuf[base - 1:base - 1 + t, :] * cw_ref[1:2, :]
            + u * cw_ref[2:3, :])
    new_halo = ubuf[base + t - halo:base + t, :]
    ubuf[base - halo:base, :] = new_halo
    z = proj(3)
    gated = (z * jax.nn.sigmoid(z)) * proj(0) * conv
    y = jnp.dot(gated.astype(BF16), wout_ref[...], preferred_element_type=F32)
    xo_ref[0] = x + y

    @pl.when(j == pl.num_programs(1) - 1)
    def _():
        st_ref[0] = new_halo


def _conv_layer(x, prev, g, w_in, conv_w, w_out, tile):
    b, s, d = x.shape
    n = s // tile
    return pl.pallas_call(
        _conv_kernel,
        grid=(b, n),
        in_specs=[
            pl.BlockSpec((1, tile, d), lambda i, j: (i, j, 0)),
            pl.BlockSpec((1, CONV_WIDTH - 1, d), lambda i, j: (i, 0, 0)),
            _const_spec((1, d)),
            _const_spec(w_in.shape),
            _const_spec(conv_w.shape),
            _const_spec(w_out.shape),
        ],
        out_specs=[
            pl.BlockSpec((1, tile, d), lambda i, j: (i, j, 0)),
            pl.BlockSpec((1, CONV_WIDTH - 1, d), lambda i, j: (i, 0, 0)),
        ],
        out_shape=[
            jax.ShapeDtypeStruct((b, s, d), F32),
            jax.ShapeDtypeStruct((b, CONV_WIDTH - 1, d), F32),
        ],
        scratch_shapes=[pltpu.VMEM((tile + 8, d), F32)],
        compiler_params=pltpu.CompilerParams(
            dimension_semantics=("arbitrary", "arbitrary"),
            vmem_limit_bytes=VMEM_LIMIT),
        name="conv_layer",
    )(x, prev, g, w_in, conv_w, w_out)


def _rope_table_kernel(inv_ref, cos_ref, sin_ref, *, offset):
    t = cos_ref.shape[0]
    row = lax.broadcasted_iota(jnp.int32, (t, LANES), 0)
    lane = lax.broadcasted_iota(jnp.int32, (t, LANES), 1)
    pos = (offset + pl.program_id(0) * t + row).astype(F32)
    ang = pos * inv_ref[...]
    cos_ref[...] = jnp.cos(ang)
    s = jnp.sin(ang)
    sin_ref[...] = jnp.where(lane < ROPE_MID, -s, s)


def _rope_tables(inv_lane, s, offset, tile):
    return pl.pallas_call(
        functools.partial(_rope_table_kernel, offset=offset),
        grid=(s // tile,),
        in_specs=[_const_spec((1, LANES))],
        out_specs=[pl.BlockSpec((tile, LANES), lambda i: (i, 0))] * 2,
        out_shape=[jax.ShapeDtypeStruct((s, LANES), F32)] * 2,
        compiler_params=pltpu.CompilerParams(dimension_semantics=("arbitrary",)),
        name="rope_tables",
    )(inv_lane)


def _rope(v, cos, sin_signed, lane):
    half = QK_ROPE_DIM // 2
    partner = jnp.where(lane < ROPE_MID,
                        pltpu.roll(v, LANES - half, 1),
                        pltpu.roll(v, half, 1))
    return v * cos + partner * sin_signed


def _expand_kv(ckv, kr_block, wk_ref, wvt_ref, k_ref, vt_ref):
    ckv_b = ckv.astype(BF16)
    t = ckv.shape[0]
    kn = jnp.dot(ckv_b, wk_ref[...], preferred_element_type=F32)
    for h in range(N_HEADS):
        k_ref[0, h] = (kn[:, h * HEAD_BLOCK:(h + 1) * HEAD_BLOCK] + kr_block).astype(BF16)
    vt = jnp.dot(wvt_ref[...], ckv.T.astype(BF16), preferred_element_type=F32)
    ones_row = (lax.broadcasted_iota(jnp.int32, (VT_ROWS, t), 0) == V_HEAD_DIM).astype(F32)
    for h in range(N_HEADS):
        vt_ref[0, h, 0] = (vt[h * VT_ROWS:(h + 1) * VT_ROWS, :] + ones_row).astype(BF16)


def _vt_spec(tile):
    return pl.BlockSpec((1, N_HEADS, 1, VT_ROWS, tile), lambda i, j: (i, 0, j, 0, 0))


def _mla_project_kernel(x_ref, g_ref, win_ref, qg_ref, wqb_ref, kvg_ref, wk_ref, wvt_ref,
                        cos_ref, sin_ref, q_ref, zg_ref, ckv_ref, kr_ref, *kv_refs):
    t = x_ref.shape[1]
    hb = _rms(x_ref[0], g_ref[...]).astype(BF16)
    o_kv = Q_LORA_RANK
    o_kr = o_kv + KV_LORA_RANK
    o_z = o_kr + HEAD_BLOCK

    def proj(lo, hi):
        return jnp.dot(hb, win_ref[:, lo:hi], preferred_element_type=F32)

    z = proj(o_z, o_z + MLA_GATE_DIM)
    zg_ref[0] = (z * jax.nn.sigmoid(z)).astype(BF16)

    ckv = _rms(proj(o_kv, o_kr), kvg_ref[...])
    ckv_ref[0] = ckv

    cos = cos_ref[...]
    sin_signed = sin_ref[...]
    lane = lax.broadcasted_iota(jnp.int32, (t, LANES), 1)
    kr_block = _rope(proj(o_kr, o_z), cos, sin_signed, lane)
    kr_ref[0] = kr_block[:, ROPE_LO:ROPE_HI]

    qn = _rms(proj(0, o_kv), qg_ref[...]).astype(BF16)
    q = jnp.dot(qn, wqb_ref[...], preferred_element_type=F32)
    for h in range(N_HEADS):
        qh = _rope(q[:, h * HEAD_BLOCK:(h + 1) * HEAD_BLOCK], cos, sin_signed, lane)
        q_ref[0, h] = (qh * Q_SCALE).astype(BF16)

    if kv_refs:
        _expand_kv(ckv, kr_block, wk_ref, wvt_ref, *kv_refs)


def _mla_project(x, g, w_in2, qg, wqb, kvg, wk, wvt, cos, sin_signed, tile, emit_kv):
    b, s, d = x.shape
    n = s // tile
    tok = lambda w: pl.BlockSpec((1, tile, w), lambda i, j: (i, j, 0))
    head = lambda nh: pl.BlockSpec((1, nh, tile, LANES), lambda i, j: (i, 0, j, 0))
    out_specs = [head(N_HEADS), tok(MLA_GATE_DIM), tok(KV_LORA_RANK), tok(QK_ROPE_DIM)]
    out_shape = [
        jax.ShapeDtypeStruct((b, N_HEADS, s, HEAD_BLOCK), BF16),
        jax.ShapeDtypeStruct((b, s, MLA_GATE_DIM), BF16),
        jax.ShapeDtypeStruct((b, s, KV_LORA_RANK), F32),
        jax.ShapeDtypeStruct((b, s, QK_ROPE_DIM), F32),
    ]
    if emit_kv:
        out_specs += [head(N_HEADS), _vt_spec(tile)]
        out_shape += [jax.ShapeDtypeStruct((b, N_HEADS, s, HEAD_BLOCK), BF16),
                      jax.ShapeDtypeStruct((b, N_HEADS, n, VT_ROWS, tile), BF16)]
    return pl.pallas_call(
        _mla_project_kernel,
        grid=(b, n),
        in_specs=[
            tok(d),
            _const_spec((1, d)),
            _const_spec(w_in2.shape),
            _const_spec(qg.shape),
            _const_spec(wqb.shape),
            _const_spec(kvg.shape),
            _const_spec(wk.shape),
            _const_spec(wvt.shape),
            pl.BlockSpec((tile, LANES), lambda i, j: (j, 0)),
            pl.BlockSpec((tile, LANES), lambda i, j: (j, 0)),
        ],
        out_specs=out_specs,
        out_shape=out_shape,
        compiler_params=pltpu.CompilerParams(
            dimension_semantics=("arbitrary", "arbitrary"),
            vmem_limit_bytes=VMEM_LIMIT),
        name="mla_project",
    )(x, g, w_in2, qg, wqb, kvg, wk, wvt, cos, sin_signed)


def _mla_expand_kernel(ckv_ref, kr_ref, place_ref, wk_ref, wvt_ref, k_ref, v_ref):
    kr_block = jnp.dot(kr_ref[0].astype(BF16), place_ref[...], preferred_element_type=F32)
    _expand_kv(ckv_ref[0], kr_block, wk_ref, wvt_ref, k_ref, v_ref)


def _mla_expand(ckv, kr, place, wk, wvt, tile):
    b, s, _ = ckv.shape
    n = s // tile
    tok = lambda w: pl.BlockSpec((1, tile, w), lambda i, j: (i, j, 0))
    head = lambda nh: pl.BlockSpec((1, nh, tile, LANES), lambda i, j: (i, 0, j, 0))
    return pl.pallas_call(
        _mla_expand_kernel,
        grid=(b, n),
        in_specs=[tok(KV_LORA_RANK), tok(QK_ROPE_DIM), _const_spec(place.shape),
                  _const_spec(wk.shape), _const_spec(wvt.shape)],
        out_specs=[head(N_HEADS), _vt_spec(tile)],
        out_shape=[jax.ShapeDtypeStruct((b, N_HEADS, s, HEAD_BLOCK), BF16),
                   jax.ShapeDtypeStruct((b, N_HEADS, n, VT_ROWS, tile), BF16)],
        compiler_params=pltpu.CompilerParams(
            dimension_semantics=("arbitrary", "arbitrary"),
            vmem_limit_bytes=VMEM_LIMIT),
        name="mla_expand",
    )(ckv, kr, place, wk, wvt)


def _attn_kernel(q_ref, k_ref, vt_ref, o_ref, m_sc, l_sc, acc_sc, s_sc, *, causal, kv_len):
    n_group, tq = q_ref.shape[1], q_ref.shape[2]
    tk = vt_ref.shape[4]
    n_full = pl.program_id(2) if causal else kv_len // tk

    m_sc[...] = jnp.full(m_sc.shape, NEG_INF, F32)
    l_sc[...] = jnp.zeros(l_sc.shape, F32)
    acc_sc[...] = jnp.zeros(acc_sc.shape, F32)

    def last_tile_mask():
        key = lax.broadcasted_iota(jnp.int32, (tk, tq), 0)
        if causal:
            qry = lax.broadcasted_iota(jnp.int32, (tk, tq), 1)
            return (key // CHUNK) <= (qry // CHUNK)
        return key < (kv_len - (kv_len // tk) * tk)

    def scores(h, j):
        k = k_ref[0, h, pl.ds(pl.multiple_of(j * tk, tk), tk), :]
        return lax.dot_general(k, q_ref[0, h], (((1,), (1,)), ((), ())),
                               preferred_element_type=F32)

    def tile(j, mask):
        ahead = s_sc.shape[0]
        issued = {}
        for h in range(n_group):
            pr, lo = h // 2, (h % 2) * V_HEAD_DIM
            st = s_sc[h] if h < ahead else issued.pop(h)
            if h + ahead < n_group:
                issued[h + ahead] = scores(h + ahead, j)
            elif mask is None:
                s_sc[h + ahead - n_group] = scores(h + ahead - n_group, j + 1)
            if mask is not None:
                st = jnp.where(mask, st, NEG_INF)
            m_prev = m_sc[h]
            m_new = jnp.maximum(m_prev, jnp.max(st, axis=0, keepdims=True))
            alpha = jnp.exp2(m_prev - m_new)
            pt = jnp.exp2(st - m_new).astype(BF16)
            pv = jnp.dot(vt_ref[0, h, j], pt, preferred_element_type=F32)
            l_sc[h] = alpha * l_sc[h] + pv[V_HEAD_DIM:V_HEAD_DIM + 1, :]
            m_sc[h] = m_new
            acc_sc[pr, lo:lo + V_HEAD_DIM, :] = (acc_sc[pr, lo:lo + V_HEAD_DIM, :] * alpha
                                                 + pv[0:V_HEAD_DIM, :])

    def body(j, carry):
        tile(j, None)
        return carry

    for h in range(s_sc.shape[0]):
        s_sc[h] = scores(h, 0)
    lax.fori_loop(0, n_full, body, 0)
    tile(n_full, last_tile_mask())

    for pr in range(n_group // 2):
        inv_l = jnp.concatenate(
            [jnp.broadcast_to(1.0 / l_sc[2 * pr + hh], (V_HEAD_DIM, tq)) for hh in range(2)],
            axis=0)
        o_ref[0, pr] = (acc_sc[pr] * inv_l).T.astype(o_ref.dtype)


def _attention(q, k, vt, tq, group, causal, kv_len):
    b, nh, sq, _ = q.shape
    skv = k.shape[2]
    n_kv, tk = vt.shape[2], vt.shape[4]
    assert n_kv * tk == skv
    assert (tq == tk and skv == sq) if causal else kv_len % tk != 0
    gp = group // 2
    return pl.pallas_call(
        functools.partial(_attn_kernel, causal=causal, kv_len=kv_len),
        grid=(b, nh // group, sq // tq),
        in_specs=[
            pl.BlockSpec((1, group, tq, HEAD_BLOCK), lambda i, g, j: (i, g, j, 0)),
            pl.BlockSpec((1, group, skv, HEAD_BLOCK), lambda i, g, j: (i, g, 0, 0)),
            pl.BlockSpec((1, group, n_kv, VT_ROWS, tk), lambda i, g, j: (i, g, 0, 0, 0)),
        ],
        out_specs=pl.BlockSpec((1, gp, tq, LANES), lambda i, g, j: (i, g, j, 0)),
        out_shape=jax.ShapeDtypeStruct((b, nh // 2, sq, LANES), BF16),
        scratch_shapes=[pltpu.VMEM((group, 1, tq), F32),
                        pltpu.VMEM((group, 1, tq), F32),
                        pltpu.VMEM((gp, LANES, tq), F32),
                        pltpu.VMEM((SCORES_AHEAD, tk, tq), F32)],
        compiler_params=pltpu.CompilerParams(
            dimension_semantics=("arbitrary", "arbitrary", "arbitrary"),
            vmem_limit_bytes=VMEM_LIMIT),
        name="attention",
    )(q, k, vt)


def _out_kernel(o_ref, zg_ref, x_ref, w_ref, g_ref, y_ref):
    o = jnp.concatenate([o_ref[0, p] for p in range(N_PAIRS)], axis=1)
    gated = (zg_ref[0].astype(F32) * o.astype(F32)).astype(BF16)
    x2 = x_ref[0] + jnp.dot(gated, w_ref[...], preferred_element_type=F32)
    y_ref[0] = _rms(x2, g_ref[...])


def _out_layer(o, zg, x, w_out, g, tile):
    b, s, d = x.shape
    n = s // tile
    tok = lambda w: pl.BlockSpec((1, tile, w), lambda i, j: (i, j, 0))
    return pl.pallas_call(
        _out_kernel,
        grid=(b, n),
        in_specs=[pl.BlockSpec((1, N_PAIRS, tile, LANES), lambda i, j: (i, 0, j, 0)),
                  tok(MLA_GATE_DIM), tok(d), _const_spec(w_out.shape), _const_spec((1, d))],
        out_specs=tok(d),
        out_shape=jax.ShapeDtypeStruct((b, s, d), F32),
        compiler_params=pltpu.CompilerParams(
            dimension_semantics=("arbitrary", "arbitrary"),
            vmem_limit_bytes=VMEM_LIMIT),
        name="out_layer",
    )(o, zg, x, w_out, g)


def _prep_weights(mla_w_in, mla_w_qb, mla_w_kvb):
    o_kv = Q_LORA_RANK
    o_kr = o_kv + KV_LORA_RANK
    o_z = o_kr + QK_ROPE_DIM
    d = mla_w_in.shape[0]
    kr_cols = jnp.concatenate(
        [jnp.zeros((d, ROPE_LO), F32), mla_w_in[:, o_kr:o_z],
         jnp.zeros((d, HEAD_BLOCK - ROPE_HI), F32)], axis=1)
    w_in2 = jnp.concatenate([mla_w_in[:, :o_kr], kr_cols, mla_w_in[:, o_z:]], axis=1)

    wqb = mla_w_qb.reshape(Q_LORA_RANK, N_HEADS, QK_DIM)
    wqb = jnp.pad(wqb, ((0, 0), (0, 0), (0, HEAD_BLOCK - QK_DIM)))
    wqb = wqb.reshape(Q_LORA_RANK, N_HEADS * HEAD_BLOCK)

    wkvb = mla_w_kvb.reshape(KV_LORA_RANK, N_HEADS, QK_NOPE_DIM + V_HEAD_DIM)
    wk = jnp.pad(wkvb[:, :, :QK_NOPE_DIM], ((0, 0), (0, 0), (0, HEAD_BLOCK - QK_NOPE_DIM)))
    wk = wk.reshape(KV_LORA_RANK, N_HEADS * HEAD_BLOCK)
    wvt = jnp.pad(jnp.transpose(wkvb[:, :, QK_NOPE_DIM:], (1, 2, 0)),
                  ((0, 0), (0, VT_ROWS - V_HEAD_DIM), (0, 0)))
    wvt = wvt.reshape(N_HEADS * VT_ROWS, KV_LORA_RANK)
    return w_in2.astype(BF16), wqb.astype(BF16), wk.astype(BF16), wvt.astype(BF16)


def _inv_freq_lanes():
    inv = 1.0 / (ROPE_THETA ** (jnp.arange(0, QK_ROPE_DIM, 2, dtype=F32) / QK_ROPE_DIM))
    z = lambda n: jnp.zeros((n,), F32)
    return jnp.concatenate([z(ROPE_LO), inv, inv, z(HEAD_BLOCK - ROPE_HI)])[None, :]


def _rope_placement():
    rows = jnp.arange(QK_ROPE_DIM)
    return jnp.zeros((QK_ROPE_DIM, HEAD_BLOCK), F32).at[rows, ROPE_LO + rows].set(1.0).astype(BF16)


def kernel(x_prompt, x_sample, state_conv, cache_ckv, cache_krope, norm_g, final_norm_g,
           conv_w_in, conv_w, conv_w_out, mla_w_in, mla_q_norm_g, mla_w_qb,
           mla_kv_norm_g, mla_w_kvb, mla_w_out):
    bp, s, d = x_prompt.shape
    bs, t, _ = x_sample.shape
    past = cache_ckv.shape[2]

    tok_tile = 256
    attn_tile = 512
    group = 4

    g0 = norm_g[0][None, :]
    g1 = norm_g[1][None, :]
    gf = final_norm_g[None, :]
    cw_in = conv_w_in[0].astype(BF16)
    cw_out = conv_w_out[0].astype(BF16)
    w_in2, wqb, wk, wvt = _prep_weights(mla_w_in[0], mla_w_qb[0], mla_w_kvb[0])
    qg = mla_q_norm_g[0][None, :]
    kvg = mla_kv_norm_g[0][None, :]
    w_out = mla_w_out[0].astype(BF16)
    inv_lane = _inv_freq_lanes()

    zeros_prev = jnp.zeros((bp, CONV_WIDTH - 1, d), F32)
    xp1, st_p = _conv_layer(x_prompt, zeros_prev, g0, cw_in, conv_w[0], cw_out, tok_tile)
    xs1, st_s = _conv_layer(x_sample, state_conv[0], g0, cw_in, conv_w[0], cw_out, t)

    cos_p, sin_p = _rope_tables(inv_lane, s, 0, 512)
    cos_s, sin_s = _rope_tables(inv_lane, t, past, t)

    qp, zg_p, ckv_p, kr_p, kp, vp = _mla_project(
        xp1, g1, w_in2, qg, wqb, kvg, wk, wvt, cos_p, sin_p, attn_tile, True)
    qs, zg_s, ckv_s, kr_s = _mla_project(
        xs1, g1, w_in2, qg, wqb, kvg, wk, wvt, cos_s, sin_s, t, False)

    op = _attention(qp, kp, vp, attn_tile, group, True, s)

    kv_len = past + t
    kv_pad = -(-kv_len // attn_tile) * attn_tile
    pad = kv_pad - kv_len
    ckv_all = jnp.concatenate(
        [cache_ckv[0], ckv_s, jnp.zeros((bs, pad, KV_LORA_RANK), F32)], axis=1)
    kr_all = jnp.concatenate(
        [cache_krope[0], kr_s, jnp.zeros((bs, pad, QK_ROPE_DIM), F32)], axis=1)
    ks, vs = _mla_expand(ckv_all, kr_all, _rope_placement(), wk, wvt, attn_tile)
    os_ = _attention(qs, ks, vs, t, group, False, kv_len)

    y_prompt = _out_layer(op, zg_p, xp1, w_out, gf, tok_tile)
    y_sample = _out_layer(os_, zg_s, xs1, w_out, gf, t)

    return (y_prompt, y_sample, st_p[None], ckv_p[None], kr_p[None],
            st_s[None], ckv_s[None], kr_s[None])
```

```python
import functools
import math

import jax
import jax.numpy as jnp
from jax import lax
from jax.experimental import pallas as pl
from jax.experimental.pallas import tpu as pltpu

D_MODEL = 1024
CHUNK = 64
CONV_WIDTH = 3
N_HEADS = 16
QK_NOPE_DIM = 64
QK_ROPE_DIM = 32
V_HEAD_DIM = 64
QK_DIM = QK_NOPE_DIM + QK_ROPE_DIM
Q_LORA_RANK = 256
KV_LORA_RANK = 128
MLA_GATE_DIM = N_HEADS * V_HEAD_DIM
ROPE_THETA = 10000.0
EPS = 1e-6
NEG_INF = -1e30
PAST_LEN = 4096

LANES = 128
HEAD_BLOCK = LANES
N_PAIRS = N_HEADS // 2
VT_ROWS = V_HEAD_DIM + 16
ROPE_LO = QK_NOPE_DIM
ROPE_MID = ROPE_LO + QK_ROPE_DIM // 2
ROPE_HI = ROPE_LO + QK_ROPE_DIM
Q_SCALE = (QK_DIM ** -0.5) * math.log2(math.e)

SCORES_AHEAD = 3
VMEM_LIMIT =56 * 1024 * 1024
F32 = jnp.float32
BF16 = jnp.bfloat16


def _rms(x, g):
    ms = jnp.mean(x * x, axis=-1, keepdims=True)
    return x * lax.rsqrt(ms + EPS) * g


def _const_spec(shape):
    nd = len(shape)
    return pl.BlockSpec(shape, lambda *_: (0,) * nd)


def _conv_kernel(x_ref, prev_ref, g_ref, win_ref, cw_ref, wout_ref, xo_ref, st_ref, ubuf):
    j = pl.program_id(1)
    t = x_ref.shape[1]
    halo = CONV_WIDTH - 1
    base = 8

    @pl.when(j == 0)
    def _():
        ubuf[base - halo:base, :] = prev_ref[0]

    x = x_ref[0]
    hb = _rms(x, g_ref[...]).astype(BF16)

    def proj(k):
        return jnp.dot(hb, win_ref[:, k * D_MODEL:(k + 1) * D_MODEL],
                       preferred_element_type=F32)

    u = proj(1) * proj(2)
    ubuf[base:base + t, :] = u
    conv = (ubuf[base - 2:base - 2 + t, :] * cw_ref[0:1, :]
            + ubuf[base - 1:base - 1 + t, :] * cw_ref[1:2, :]
            + u * cw_ref[2:3, :])
    new_halo = ubuf[base + t - halo:base + t, :]
    ubuf[base - halo:base, :] = new_halo
    z = proj(3)
    gated = (z * jax.nn.sigmoid(z)) * proj(0) * conv
    y = jnp.dot(gated.astype(BF16), wout_ref[...], preferred_element_type=F32)
    xo_ref[0] = x + y

    @pl.when(j == pl.num_programs(1) - 1)
    def _():
        st_ref[0] = new_halo


def _conv_layer(x, prev, g, w_in, conv_w, w_out, tile):
    b, s, d = x.shape
    n = s // tile
    return pl.pallas_call(
        _conv_kernel,
        grid=(b, n),
        in_specs=[
            pl.BlockSpec((1, tile, d), lambda i, j: (i, j, 0)),
            pl.BlockSpec((1, CONV_WIDTH - 1, d), lambda i, j: (i, 0, 0)),
            _const_spec((1, d)),
            _const_spec(w_in.shape),
            _const_spec(conv_w.shape),
            _const_spec(w_out.shape),
        ],
        out_specs=[
            pl.BlockSpec((1, tile, d), lambda i, j: (i, j, 0)),
            pl.BlockSpec((1, CONV_WIDTH - 1, d), lambda i, j: (i, 0, 0)),
        ],
        out_shape=[
            jax.ShapeDtypeStruct((b, s, d), F32),
            jax.ShapeDtypeStruct((b, CONV_WIDTH - 1, d), F32),
        ],
        scratch_shapes=[pltpu.VMEM((tile + 8, d), F32)],
        compiler_params=pltpu.CompilerParams(
            dimension_semantics=("arbitrary", "arbitrary"),
            vmem_limit_bytes=VMEM_LIMIT),
        name="conv_layer",
    )(x, prev, g, w_in, conv_w, w_out)


def _rope_table_kernel(inv_ref, cos_ref, sin_ref, *, offset):
    t = cos_ref.shape[0]
    row = lax.broadcasted_iota(jnp.int32, (t, LANES), 0)
    lane = lax.broadcasted_iota(jnp.int32, (t, LANES), 1)
    pos = (offset + pl.program_id(0) * t + row).astype(F32)
    ang = pos * inv_ref[...]
    cos_ref[...] = jnp.cos(ang)
    s = jnp.sin(ang)
    sin_ref[...] = jnp.where(lane < ROPE_MID, -s, s)


def _rope_tables(inv_lane, s, offset, tile):
    return pl.pallas_call(
        functools.partial(_rope_table_kernel, offset=offset),
        grid=(s // tile,),
        in_specs=[_const_spec((1, LANES))],
        out_specs=[pl.BlockSpec((tile, LANES), lambda i: (i, 0))] * 2,
        out_shape=[jax.ShapeDtypeStruct((s, LANES), F32)] * 2,
        compiler_params=pltpu.CompilerParams(dimension_semantics=("arbitrary",)),
        name="rope_tables",
    )(inv_lane)


def _rope(v, cos, sin_signed):
    partner = pltpu.roll(v, LANES - QK_ROPE_DIM // 2, 1)
    return v * cos + partner * sin_signed


def _expand_kv(ckv, kr_block, wk_ref, wvt_ref, k_ref, vt_ref):
    t = ckv.shape[0]
    k_in = jnp.concatenate([ckv.astype(BF16), kr_block.astype(BF16)], axis=1)
    k_all = jnp.dot(k_in, wk_ref[...], preferred_element_type=F32)
    for h in range(N_HEADS):
        k_ref[0, h] = k_all[:, h * HEAD_BLOCK:(h + 1) * HEAD_BLOCK].astype(BF16)
    vt = jnp.dot(wvt_ref[...], ckv.T.astype(BF16), preferred_element_type=F32)
    ones_row = (lax.broadcasted_iota(jnp.int32, (VT_ROWS, t), 0) == V_HEAD_DIM).astype(F32)
    for h in range(N_HEADS):
        vt_ref[0, h, 0] = (vt[h * VT_ROWS:(h + 1) * VT_ROWS, :] + ones_row).astype(BF16)


def _vt_spec(tile):
    return pl.BlockSpec((1, N_HEADS, 1, VT_ROWS, tile), lambda i, j: (i, 0, j, 0, 0))


def _mla_project_kernel(x_ref, g_ref, win_ref, qg_ref, wqb_ref, kvg_ref, wk_ref, wvt_ref,
                        cos_ref, sin_ref, q_ref, zg_ref, ckv_ref, kr_ref, *kv_refs):
    hb = _rms(x_ref[0], g_ref[...]).astype(BF16)
    o_kv = Q_LORA_RANK
    o_kr = o_kv + KV_LORA_RANK
    o_z = o_kr + HEAD_BLOCK

    def proj(lo, hi):
        return jnp.dot(hb, win_ref[:, lo:hi], preferred_element_type=F32)

    z = proj(o_z, o_z + MLA_GATE_DIM)
    zg_ref[0] = (z * jax.nn.sigmoid(z)).astype(BF16)

    ckv = _rms(proj(o_kv, o_kr), kvg_ref[...])
    ckv_ref[0] = ckv

    cos = cos_ref[...]
    sin_signed = sin_ref[...]
    kr_block = _rope(proj(o_kr, o_z), cos, sin_signed)
    kr_ref[0] = kr_block[:, ROPE_LO:ROPE_HI]

    qn = _rms(proj(0, o_kv), qg_ref[...]).astype(BF16)
    q = jnp.dot(qn, wqb_ref[...], preferred_element_type=F32)
    cos_q = cos * Q_SCALE
    sin_q = sin_signed * Q_SCALE
    for h in range(N_HEADS):
        qh = _rope(q[:, h * HEAD_BLOCK:(h + 1) * HEAD_BLOCK], cos_q, sin_q)
        q_ref[0, h] = qh.astype(BF16)

    if kv_refs:
        _expand_kv(ckv, kr_block, wk_ref, wvt_ref, *kv_refs)


def _mla_project(x, g, w_in2, qg, wqb, kvg, wk, wvt, cos, sin_signed, tile, emit_kv):
    b, s, d = x.shape
    n = s // tile
    tok = lambda w: pl.BlockSpec((1, tile, w), lambda i, j: (i, j, 0))
    head = lambda nh: pl.BlockSpec((1, nh, tile, LANES), lambda i, j: (i, 0, j, 0))
    out_specs = [head(N_HEADS), tok(MLA_GATE_DIM), tok(KV_LORA_RANK), tok(QK_ROPE_DIM)]
    out_shape = [
        jax.ShapeDtypeStruct((b, N_HEADS, s, HEAD_BLOCK), BF16),
        jax.ShapeDtypeStruct((b, s, MLA_GATE_DIM), BF16),
        jax.ShapeDtypeStruct((b, s, KV_LORA_RANK), F32),
        jax.ShapeDtypeStruct((b, s, QK_ROPE_DIM), F32),
    ]
    if emit_kv:
        out_specs += [head(N_HEADS), _vt_spec(tile)]
        out_shape += [jax.ShapeDtypeStruct((b, N_HEADS, s, HEAD_BLOCK), BF16),
                      jax.ShapeDtypeStruct((b, N_HEADS, n, VT_ROWS, tile), BF16)]
    return pl.pallas_call(
        _mla_project_kernel,
        grid=(b, n),
        in_specs=[
            tok(d),
            _const_spec((1, d)),
            _const_spec(w_in2.shape),
            _const_spec(qg.shape),
            _const_spec(wqb.shape),
            _const_spec(kvg.shape),
            _const_spec(wk.shape),
            _const_spec(wvt.shape),
            pl.BlockSpec((tile, LANES), lambda i, j: (j, 0)),
            pl.BlockSpec((tile, LANES), lambda i, j: (j, 0)),
        ],
        out_specs=out_specs,
        out_shape=out_shape,
        compiler_params=pltpu.CompilerParams(
            dimension_semantics=("arbitrary", "arbitrary"),
            vmem_limit_bytes=VMEM_LIMIT),
        name="mla_project",
    )(x, g, w_in2, qg, wqb, kvg, wk, wvt, cos, sin_signed)


def _latent_attn_kernel(q_ref, ckv_ref, kr_ref, ckvn_ref, krn_ref, place_ref, wkt_ref, wvp_ref,
                        o_ref, qa_sc, m_sc, l_sc, acc_sc):
    j = pl.program_id(1)
    t = q_ref.shape[2]
    cols = N_HEADS * t
    col_tile = 2 * LANES

    @pl.when(j == 0)
    def _():
        lane = lax.broadcasted_iota(jnp.int32, (t, LANES), 1)
        is_rope = (lane >= ROPE_LO) & (lane < ROPE_HI)
        for h in range(N_HEADS):
            qh = q_ref[0, h]
            q_lat = jnp.dot(qh, wkt_ref[h], preferred_element_type=F32)
            qa_sc[h * t:(h + 1) * t, 0:LANES] = q_lat.astype(BF16)
            qa_sc[h * t:(h + 1) * t, LANES:2 * LANES] = jnp.where(is_rope, qh, jnp.zeros_like(qh))
        m_sc[...] = jnp.full(m_sc.shape, NEG_INF, F32)
        l_sc[...] = jnp.zeros(l_sc.shape, F32)
        acc_sc[...] = jnp.zeros(acc_sc.shape, F32)

    def attend(ckv, kr):
        tk = ckv.shape[0]
        kr_block = jnp.dot(kr.astype(BF16), place_ref[...], preferred_element_type=F32)
        keys = jnp.concatenate([ckv.astype(BF16), kr_block.astype(BF16)], axis=1)
        ones_rows = (lax.broadcasted_iota(jnp.int32, (16, tk), 0) == 0).astype(F32)
        vals_t = jnp.concatenate([ckv.T, ones_rows], axis=0).astype(BF16)
        n_col = cols // col_tile
        col = lambda c: slice(c * col_tile, (c + 1) * col_tile)
        scores = lambda c: lax.dot_general(keys, qa_sc[col(c), :], (((1,), (1,)), ((), ())),
                                           preferred_element_type=F32)
        issued = {c: scores(c) for c in range(min(2, n_col))}
        for c in range(n_col):
            cs = col(c)
            st = issued.pop(c)
            if c + 2 < n_col:
                issued[c + 2] = scores(c + 2)
            m_prev = m_sc[:, cs]
            m_new = jnp.maximum(m_prev, jnp.max(st, axis=0, keepdims=True))
            alpha = jnp.exp2(m_prev - m_new)
            pt = jnp.exp2(st - m_new).astype(BF16)
            pv = jnp.dot(vals_t, pt, preferred_element_type=F32)
            l_sc[:, cs] = alpha * l_sc[:, cs] + pv[KV_LORA_RANK:KV_LORA_RANK + 1, :]
            m_sc[:, cs] = m_new
            acc_sc[:, cs] = acc_sc[:, cs] * alpha + pv[0:KV_LORA_RANK, :]

    attend(ckv_ref[0], kr_ref[0])

    @pl.when(j == pl.num_programs(1) - 1)
    def _():
        attend(ckvn_ref[0], krn_ref[0])
        o_lat = (acc_sc[...] / l_sc[...]).T.astype(BF16)
        for p in range(N_PAIRS):
            pair = sum(jnp.dot(o_lat[h * t:(h + 1) * t, :], wvp_ref[h],
                               preferred_element_type=F32) for h in (2 * p, 2 * p + 1))
            o_ref[0, p] = pair.astype(o_ref.dtype)


def _latent_attention(q, cache_ckv, cache_kr, ckv_new, kr_new, place, wkt, wvp, tile):
    b, nh, t, _ = q.shape
    past = cache_ckv.shape[1]
    cols = nh * t
    assert past % tile == 0 and cols % (2 * LANES) == 0
    tok = lambda rows, w, im: pl.BlockSpec((1, rows, w), im)
    return pl.pallas_call(
        _latent_attn_kernel,
        grid=(b, past // tile),
        in_specs=[
            pl.BlockSpec((1, nh, t, HEAD_BLOCK), lambda i, j: (i, 0, 0, 0)),
            tok(tile, KV_LORA_RANK, lambda i, j: (i, j, 0)),
            tok(tile, QK_ROPE_DIM, lambda i, j: (i, j, 0)),
            tok(t, KV_LORA_RANK, lambda i, j: (i, 0, 0)),
            tok(t, QK_ROPE_DIM, lambda i, j: (i, 0, 0)),
            _const_spec(place.shape), _const_spec(wkt.shape), _const_spec(wvp.shape),
        ],
        out_specs=pl.BlockSpec((1, N_PAIRS, t, LANES), lambda i, j: (i, 0, 0, 0)),
        out_shape=jax.ShapeDtypeStruct((b, N_PAIRS, t, LANES), BF16),
        scratch_shapes=[pltpu.VMEM((cols, 2 * LANES), BF16),
                        pltpu.VMEM((1, cols), F32),
                        pltpu.VMEM((1, cols), F32),
                        pltpu.VMEM((KV_LORA_RANK, cols), F32)],
        compiler_params=pltpu.CompilerParams(
            dimension_semantics=("arbitrary", "arbitrary"),
            vmem_limit_bytes=VMEM_LIMIT),
        name="latent_attention",
    )(q, cache_ckv, cache_kr, ckv_new, kr_new, place, wkt, wvp)


def _attn_kernel(q_ref, k_ref, vt_ref, o_ref, m_sc, l_sc, acc_sc, s_sc):
    n_group, tq = q_ref.shape[1], q_ref.shape[2]
    tk = vt_ref.shape[4]
    n_full = pl.program_id(2)

    m_sc[...] = jnp.full(m_sc.shape, NEG_INF, F32)
    l_sc[...] = jnp.zeros(l_sc.shape, F32)
    acc_sc[...] = jnp.zeros(acc_sc.shape, F32)

    def last_tile_mask():
        key = lax.broadcasted_iota(jnp.int32, (tk, tq), 0)
        qry = lax.broadcasted_iota(jnp.int32, (tk, tq), 1)
        return (key // CHUNK) <= (qry // CHUNK)

    def scores(h, j):
        k = k_ref[0, h, pl.ds(pl.multiple_of(j * tk, tk), tk), :]
        return lax.dot_general(k, q_ref[0, h], (((1,), (1,)), ((), ())),
                               preferred_element_type=F32)

    def tile(j, mask):
        ahead = s_sc.shape[0]
        issued = {}
        for h in range(n_group):
            pr, lo = h // 2, (h % 2) * V_HEAD_DIM
            st = s_sc[h] if h < ahead else issued.pop(h)
            if h + ahead < n_group:
                issued[h + ahead] = scores(h + ahead, j)
            elif mask is None:
                s_sc[h + ahead - n_group] = scores(h + ahead - n_group, j + 1)
            if mask is not None:
                st = jnp.where(mask, st, NEG_INF)
            m_prev = m_sc[h]
            m_new = jnp.maximum(m_prev, jnp.max(st, axis=0, keepdims=True))
            alpha = jnp.exp2(m_prev - m_new)
            pt = jnp.exp2(st - m_new).astype(BF16)
            pv = jnp.dot(vt_ref[0, h, j], pt, preferred_element_type=F32)
            l_sc[h] = alpha * l_sc[h] + pv[V_HEAD_DIM:V_HEAD_DIM + 1, :]
            m_sc[h] = m_new
            acc_sc[pr, lo:lo + V_HEAD_DIM, :] = (acc_sc[pr, lo:lo + V_HEAD_DIM, :] * alpha
                                                 + pv[0:V_HEAD_DIM, :])

    def body2(jj, carry):
        tile(2 * jj, None)
        tile(2 * jj + 1, None)
        return carry

    def body1(j, carry):
        tile(j, None)
        return carry

    for h in range(s_sc.shape[0]):
        s_sc[h] = scores(h, 0)
    n_pairs = n_full // 2
    lax.fori_loop(0, n_pairs, body2, 0)
    lax.fori_loop(2 * n_pairs, n_full, body1, 0)
    tile(n_full, last_tile_mask())

    for pr in range(n_group // 2):
        inv_l = jnp.concatenate(
            [jnp.broadcast_to(1.0 / l_sc[2 * pr + hh], (V_HEAD_DIM, tq)) for hh in range(2)],
            axis=0)
        o_ref[0, pr] = (acc_sc[pr] * inv_l).T.astype(o_ref.dtype)


def _attention(q, k, vt, group):
    b, nh, sq, _ = q.shape
    skv = k.shape[2]
    n_kv, tk = vt.shape[2], vt.shape[4]
    tq = tk
    assert n_kv * tk == skv == sq
    gp = group // 2
    return pl.pallas_call(
        _attn_kernel,
        grid=(b, nh // group, sq // tq),
        in_specs=[
            pl.BlockSpec((1, group, tq, HEAD_BLOCK), lambda i, g, j: (i, g, j, 0)),
            pl.BlockSpec((1, group, skv, HEAD_BLOCK), lambda i, g, j: (i, g, 0, 0)),
            pl.BlockSpec((1, group, n_kv, VT_ROWS, tk), lambda i, g, j: (i, g, 0, 0, 0)),
        ],
        out_specs=pl.BlockSpec((1, gp, tq, LANES), lambda i, g, j: (i, g, j, 0)),
        out_shape=jax.ShapeDtypeStruct((b, nh // 2, sq, LANES), BF16),
        scratch_shapes=[pltpu.VMEM((group, 1, tq), F32),
                        pltpu.VMEM((group, 1, tq), F32),
                        pltpu.VMEM((gp, LANES, tq), F32),
                        pltpu.VMEM((SCORES_AHEAD, tk, tq), F32)],
        compiler_params=pltpu.CompilerParams(
            dimension_semantics=("arbitrary", "arbitrary", "arbitrary"),
            vmem_limit_bytes=VMEM_LIMIT),
        name="attention",
    )(q, k, vt)


def _out_kernel(o_ref, zg_ref, x_ref, w_ref, g_ref, y_ref):
    o = jnp.concatenate([o_ref[0, p] for p in range(N_PAIRS)], axis=1)
    gated = (zg_ref[0].astype(F32) * o.astype(F32)).astype(BF16)
    x2 = x_ref[0] + jnp.dot(gated, w_ref[...], preferred_element_type=F32)
    y_ref[0] = _rms(x2, g_ref[...])


def _out_layer(o, zg, x, w_out, g, tile):
    b, s, d = x.shape
    n = s // tile
    tok = lambda w: pl.BlockSpec((1, tile, w), lambda i, j: (i, j, 0))
    return pl.pallas_call(
        _out_kernel,
        grid=(b, n),
        in_specs=[pl.BlockSpec((1, N_PAIRS, tile, LANES), lambda i, j: (i, 0, j, 0)),
                  tok(MLA_GATE_DIM), tok(d), _const_spec(w_out.shape), _const_spec((1, d))],
        out_specs=tok(d),
        out_shape=jax.ShapeDtypeStruct((b, s, d), F32),
        compiler_params=pltpu.CompilerParams(
            dimension_semantics=("arbitrary", "arbitrary"),
            vmem_limit_bytes=VMEM_LIMIT),
        name="out_layer",
    )(o, zg, x, w_out, g)


def _rope_block_cols(w):
    half = QK_ROPE_DIM // 2
    fill = jnp.zeros(w.shape[:-1] + (HEAD_BLOCK - ROPE_HI - half,), w.dtype)
    return jnp.concatenate([w, w[..., :half], fill], axis=-1)


def _prep_weights(mla_w_in, mla_w_qb, mla_w_kvb):
    o_kv = Q_LORA_RANK
    o_kr = o_kv + KV_LORA_RANK
    o_z = o_kr + QK_ROPE_DIM
    d = mla_w_in.shape[0]
    kr_cols = jnp.concatenate(
        [jnp.zeros((d, ROPE_LO), F32), _rope_block_cols(mla_w_in[:, o_kr:o_z])], axis=1)
    w_in2 = jnp.concatenate([mla_w_in[:, :o_kr], kr_cols, mla_w_in[:, o_z:]], axis=1)

    wqb = mla_w_qb.reshape(Q_LORA_RANK, N_HEADS, QK_DIM)
    wqb = jnp.concatenate([wqb[..., :QK_NOPE_DIM], _rope_block_cols(wqb[..., QK_NOPE_DIM:])], axis=-1)
    wqb = wqb.reshape(Q_LORA_RANK, N_HEADS * HEAD_BLOCK)

    wkvb = mla_w_kvb.reshape(KV_LORA_RANK, N_HEADS, QK_NOPE_DIM + V_HEAD_DIM)
    wk_pad = jnp.pad(wkvb[:, :, :QK_NOPE_DIM], ((0, 0), (0, 0), (0, HEAD_BLOCK - QK_NOPE_DIM)))
    place = jnp.zeros((HEAD_BLOCK, N_HEADS, HEAD_BLOCK), F32)
    lanes = jnp.arange(ROPE_LO, ROPE_HI)
    place = place.at[lanes, :, lanes].set(1.0)
    wk = jnp.concatenate([wk_pad, place], axis=0).reshape(KV_LORA_RANK + HEAD_BLOCK,
                                                          N_HEADS * HEAD_BLOCK)
    wv = wkvb[:, :, QK_NOPE_DIM:]
    wvt = jnp.pad(jnp.transpose(wv, (1, 2, 0)), ((0, 0), (0, VT_ROWS - V_HEAD_DIM), (0, 0)))
    wvt = wvt.reshape(N_HEADS * VT_ROWS, KV_LORA_RANK)

    wkt = jnp.transpose(wk_pad, (1, 2, 0))
    wv_lo = jnp.pad(wv, ((0, 0), (0, 0), (0, LANES - V_HEAD_DIM)))
    wv_hi = jnp.pad(wv, ((0, 0), (0, 0), (LANES - V_HEAD_DIM, 0)))
    odd = (jnp.arange(N_HEADS) % 2 == 1)[None, :, None]
    wvp = jnp.transpose(jnp.where(odd, wv_hi, wv_lo), (1, 0, 2))
    return tuple(w.astype(BF16) for w in (w_in2, wqb, wk, wvt, wkt, wvp))


def _inv_freq_lanes():
    inv = 1.0 / (ROPE_THETA ** (jnp.arange(0, QK_ROPE_DIM, 2, dtype=F32) / QK_ROPE_DIM))
    z = lambda n: jnp.zeros((n,), F32)
    return jnp.concatenate([z(ROPE_LO), inv, inv, z(HEAD_BLOCK - ROPE_HI)])[None, :]


def _rope_placement():
    rows = jnp.arange(QK_ROPE_DIM)
    return jnp.zeros((QK_ROPE_DIM, HEAD_BLOCK), F32).at[rows, ROPE_LO + rows].set(1.0).astype(BF16)


def kernel(x_prompt, x_sample, state_conv, cache_ckv, cache_krope, norm_g, final_norm_g,
           conv_w_in, conv_w, conv_w_out, mla_w_in, mla_q_norm_g, mla_w_qb,
           mla_kv_norm_g, mla_w_kvb, mla_w_out):
    bp, s, d = x_prompt.shape
    bs, t, _ = x_sample.shape
    past = cache_ckv.shape[2]

    tok_tile = 256
    attn_tile = 512
    group = 4

    g0 = norm_g[0][None, :]
    g1 = norm_g[1][None, :]
    gf = final_norm_g[None, :]
    cw_in = conv_w_in[0].astype(BF16)
    cw_out = conv_w_out[0].astype(BF16)
    w_in2, wqb, wk, wvt, wkt, wvp = _prep_weights(mla_w_in[0], mla_w_qb[0], mla_w_kvb[0])
    qg = mla_q_norm_g[0][None, :]
    kvg = mla_kv_norm_g[0][None, :]
    w_out = mla_w_out[0].astype(BF16)
    inv_lane = _inv_freq_lanes()

    zeros_prev = jnp.zeros((bp, CONV_WIDTH - 1, d), F32)
    xp1, st_p = _conv_layer(x_prompt, zeros_prev, g0, cw_in, conv_w[0], cw_out, tok_tile)
    xs1, st_s = _conv_layer(x_sample, state_conv[0], g0, cw_in, conv_w[0], cw_out, t)

    cos_p, sin_p = _rope_tables(inv_lane, s, 0, 512)
    cos_s, sin_s = _rope_tables(inv_lane, t, past, t)

    qp, zg_p, ckv_p, kr_p, kp, vp = _mla_project(
        xp1, g1, w_in2, qg, wqb, kvg, wk, wvt, cos_p, sin_p, attn_tile, True)
    qs, zg_s, ckv_s, kr_s = _mla_project(
        xs1, g1, w_in2, qg, wqb, kvg, wk, wvt, cos_s, sin_s, t, False)

    op = _attention(qp, kp, vp, group)
    os_ = _latent_attention(qs, cache_ckv[0], cache_krope[0], ckv_s, kr_s,
                            _rope_placement(), wkt, wvp, attn_tile)

    y_prompt = _out_layer(op, zg_p, xp1, w_out, gf, tok_tile)
    y_sample = _out_layer(os_, zg_s, xs1, w_out, gf, t)

    return (y_prompt, y_sample, st_p[None], ckv_p[None], kr_p[None],
            st_s[None], ckv_s[None], kr_s[None])
```

```python
import functools
import math

import jax
import jax.numpy as jnp
import numpy as np
from jax import lax
from jax.experimental import pallas as pl
from jax.experimental.pallas import tpu as pltpu

D_MODEL = 1024
CHUNK = 64
CONV_WIDTH = 3
N_HEADS = 16
QK_NOPE_DIM = 64
QK_ROPE_DIM = 32
V_HEAD_DIM = 64
QK_DIM = QK_NOPE_DIM + QK_ROPE_DIM
Q_LORA_RANK = 256
KV_LORA_RANK = 128
MLA_GATE_DIM = N_HEADS * V_HEAD_DIM
ROPE_THETA = 10000.0
EPS = 1e-6
NEG_INF = -1e30
PAST_LEN = 4096

LANES = 128
HEAD_BLOCK = LANES
N_PAIRS = N_HEADS // 2
VT_ROWS = V_HEAD_DIM + 16
ROPE_LO = QK_NOPE_DIM
ROPE_MID = ROPE_LO + QK_ROPE_DIM // 2
ROPE_HI = ROPE_LO + QK_ROPE_DIM
Q_SCALE = (QK_DIM ** -0.5) * math.log2(math.e)

VMEM_LIMIT =56 * 1024 * 1024
F32 = jnp.float32
BF16 = jnp.bfloat16


def _rms(x, g):
    ms = jnp.mean(x * x, axis=-1, keepdims=True)
    return x * lax.rsqrt(ms + EPS) * g


def _const_spec(shape):
    nd = len(shape)
    return pl.BlockSpec(shape, lambda *_: (0,) * nd)


def _conv_kernel(x_ref, prev_ref, g_ref, win_ref, cw_ref, wout_ref, xo_ref, st_ref, ubuf):
    j = pl.program_id(1)
    t = x_ref.shape[1]
    halo = CONV_WIDTH - 1
    base = 8

    @pl.when(j == 0)
    def _():
        ubuf[base - halo:base, :] = prev_ref[0]

    x = x_ref[0]
    hb = _rms(x, g_ref[...]).astype(BF16)

    def proj(k):
        return jnp.dot(hb, win_ref[:, k * D_MODEL:(k + 1) * D_MODEL],
                       preferred_element_type=F32)

    u = proj(1) * proj(2)
    ubuf[base:base + t, :] = u
    conv = (ubuf[base - 2:base - 2 + t, :] * cw_ref[0:1, :]
            + ubuf[base - 1:base - 1 + t, :] * cw_ref[1:2, :]
            + u * cw_ref[2:3, :])
    new_halo = ubuf[base + t - halo:base + t, :]
    ubuf[base - halo:base, :] = new_halo
    z = proj(3)
    gated = (z * jax.nn.sigmoid(z)) * proj(0) * conv
    y = jnp.dot(gated.astype(BF16), wout_ref[...], preferred_element_type=F32)
    xo_ref[0] = x + y

    @pl.when(j == pl.num_programs(1) - 1)
    def _():
        st_ref[0] = new_halo


def _conv_layer(x, prev, g, w_in, conv_w, w_out, tile):
    b, s, d = x.shape
    n = s // tile
    return pl.pallas_call(
        _conv_kernel,
        grid=(b, n),
        in_specs=[
            pl.BlockSpec((1, tile, d), lambda i, j: (i, j, 0)),
            pl.BlockSpec((1, CONV_WIDTH - 1, d), lambda i, j: (i, 0, 0)),
            _const_spec((1, d)),
            _const_spec(w_in.shape),
            _const_spec(conv_w.shape),
            _const_spec(w_out.shape),
        ],
        out_specs=[
            pl.BlockSpec((1, tile, d), lambda i, j: (i, j, 0)),
            pl.BlockSpec((1, CONV_WIDTH - 1, d), lambda i, j: (i, 0, 0)),
        ],
        out_shape=[
            jax.ShapeDtypeStruct((b, s, d), F32),
            jax.ShapeDtypeStruct((b, CONV_WIDTH - 1, d), F32),
        ],
        scratch_shapes=[pltpu.VMEM((tile + 8, d), F32)],
        compiler_params=pltpu.CompilerParams(
            dimension_semantics=("arbitrary", "arbitrary"),
            vmem_limit_bytes=VMEM_LIMIT),
        name="conv_layer",
    )(x, prev, g, w_in, conv_w, w_out)


def _rope_table_kernel(inv_ref, cos_ref, sin_ref, *, offset):
    t = cos_ref.shape[0]
    row = lax.broadcasted_iota(jnp.int32, (t, LANES), 0)
    lane = lax.broadcasted_iota(jnp.int32, (t, LANES), 1)
    pos = (offset + pl.program_id(0) * t + row).astype(F32)
    ang = pos * inv_ref[...]
    cos_ref[...] = jnp.cos(ang)
    s = jnp.sin(ang)
    sin_ref[...] = jnp.where(lane < ROPE_MID, -s, s)


def _rope_tables(inv_lane, s, offset, tile):
    return pl.pallas_call(
        functools.partial(_rope_table_kernel, offset=offset),
        grid=(s // tile,),
        in_specs=[_const_spec((1, LANES))],
        out_specs=[pl.BlockSpec((tile, LANES), lambda i: (i, 0))] * 2,
        out_shape=[jax.ShapeDtypeStruct((s, LANES), F32)] * 2,
        compiler_params=pltpu.CompilerParams(dimension_semantics=("arbitrary",)),
        name="rope_tables",
    )(inv_lane)


def _rope(v, cos, sin_signed):
    partner = pltpu.roll(v, LANES - QK_ROPE_DIM // 2, 1)
    return v * cos + partner * sin_signed


def _expand_kv(ckv, kr_block, wk_ref, wvt_ref, k_ref, vt_ref):
    t = ckv.shape[0]
    k_in = jnp.concatenate([ckv.astype(BF16), kr_block.astype(BF16)], axis=1)
    k_all = jnp.dot(k_in, wk_ref[...], preferred_element_type=F32)
    for h in range(N_HEADS):
        k_ref[0, h] = k_all[:, h * HEAD_BLOCK:(h + 1) * HEAD_BLOCK].astype(BF16)
    vt = jnp.dot(wvt_ref[...], ckv.T.astype(BF16), preferred_element_type=F32)
    ones_row = (lax.broadcasted_iota(jnp.int32, (VT_ROWS, t), 0) == V_HEAD_DIM).astype(F32)
    for h in range(N_HEADS):
        vt_ref[0, h, 0] = (vt[h * VT_ROWS:(h + 1) * VT_ROWS, :] + ones_row).astype(BF16)


def _vt_spec(tile):
    return pl.BlockSpec((1, N_HEADS, 1, VT_ROWS, tile), lambda i, j: (i, 0, j, 0, 0))


def _mla_project_kernel(x_ref, g_ref, win_ref, qg_ref, wqb_ref, kvg_ref, wk_ref, wvt_ref,
                        cos_ref, sin_ref, q_ref, zg_ref, ckv_ref, kr_ref, *kv_refs):
    hb = _rms(x_ref[0], g_ref[...]).astype(BF16)
    o_kv = Q_LORA_RANK
    o_kr = o_kv + KV_LORA_RANK
    o_z = o_kr + HEAD_BLOCK

    def proj(lo, hi):
        return jnp.dot(hb, win_ref[:, lo:hi], preferred_element_type=F32)

    z = proj(o_z, o_z + MLA_GATE_DIM)
    zg_ref[0] = (z * jax.nn.sigmoid(z)).astype(BF16)

    ckv = _rms(proj(o_kv, o_kr), kvg_ref[...])
    ckv_ref[0] = ckv

    cos = cos_ref[...]
    sin_signed = sin_ref[...]
    kr_block = _rope(proj(o_kr, o_z), cos, sin_signed)
    kr_ref[0] = kr_block.T[ROPE_LO:ROPE_HI, :]

    qn = _rms(proj(0, o_kv), qg_ref[...]).astype(BF16)
    q = jnp.dot(qn, wqb_ref[...], preferred_element_type=F32)
    cos_q = cos * Q_SCALE
    sin_q = sin_signed * Q_SCALE
    for h in range(N_HEADS):
        qh = _rope(q[:, h * HEAD_BLOCK:(h + 1) * HEAD_BLOCK], cos_q, sin_q)
        q_ref[0, h] = qh.astype(BF16)

    if kv_refs:
        _expand_kv(ckv, kr_block, wk_ref, wvt_ref, *kv_refs)


def _mla_project(x, g, w_in2, qg, wqb, kvg, wk, wvt, cos, sin_signed, tile, emit_kv):
    b, s, d = x.shape
    n = s // tile
    tok = lambda w: pl.BlockSpec((1, tile, w), lambda i, j: (i, j, 0))
    head = lambda nh: pl.BlockSpec((1, nh, tile, LANES), lambda i, j: (i, 0, j, 0))
    out_specs = [head(N_HEADS), tok(MLA_GATE_DIM), tok(KV_LORA_RANK),
                 pl.BlockSpec((1, QK_ROPE_DIM, tile), lambda i, j: (i, 0, j))]
    out_shape = [
        jax.ShapeDtypeStruct((b, N_HEADS, s, HEAD_BLOCK), BF16),
        jax.ShapeDtypeStruct((b, s, MLA_GATE_DIM), BF16),
        jax.ShapeDtypeStruct((b, s, KV_LORA_RANK), F32),
        jax.ShapeDtypeStruct((b, QK_ROPE_DIM, s), F32),
    ]
    if emit_kv:
        out_specs += [head(N_HEADS), _vt_spec(tile)]
        out_shape += [jax.ShapeDtypeStruct((b, N_HEADS, s, HEAD_BLOCK), BF16),
                      jax.ShapeDtypeStruct((b, N_HEADS, n, VT_ROWS, tile), BF16)]
    return pl.pallas_call(
        _mla_project_kernel,
        grid=(b, n),
        in_specs=[
            tok(d),
            _const_spec((1, d)),
            _const_spec(w_in2.shape),
            _const_spec(qg.shape),
            _const_spec(wqb.shape),
            _const_spec(kvg.shape),
            _const_spec(wk.shape),
            _const_spec(wvt.shape),
            pl.BlockSpec((tile, LANES), lambda i, j: (j, 0)),
            pl.BlockSpec((tile, LANES), lambda i, j: (j, 0)),
        ],
        out_specs=out_specs,
        out_shape=out_shape,
        compiler_params=pltpu.CompilerParams(
            dimension_semantics=("arbitrary", "arbitrary"),
            vmem_limit_bytes=VMEM_LIMIT),
        name="mla_project",
    )(x, g, w_in2, qg, wqb, kvg, wk, wvt, cos, sin_signed)


def _latent_attn_kernel(q_ref, ckv_ref, kr_ref, ckvn_ref, krn_ref, place_ref, wkt_ref, wvp_ref,
                        o_ref, qa_sc, m_sc, l_sc, acc_sc):
    j = pl.program_id(1)
    t = q_ref.shape[2]
    cols = N_HEADS * t
    col_tile = 2 * LANES

    @pl.when(j == 0)
    def _():
        lane = lax.broadcasted_iota(jnp.int32, (t, LANES), 1)
        is_rope = (lane >= ROPE_LO) & (lane < ROPE_HI)
        for h in range(N_HEADS):
            qh = q_ref[0, h]
            q_lat = jnp.dot(qh, wkt_ref[h], preferred_element_type=F32)
            qa_sc[h * t:(h + 1) * t, 0:LANES] = q_lat.astype(BF16)
            qa_sc[h * t:(h + 1) * t, LANES:2 * LANES] = jnp.where(is_rope, qh, jnp.zeros_like(qh))
        m_sc[...] = jnp.full(m_sc.shape, NEG_INF, F32)
        l_sc[...] = jnp.zeros(l_sc.shape, F32)
        acc_sc[...] = jnp.zeros(acc_sc.shape, F32)

    def attend(ckv, kr_t):
        tk = ckv.shape[0]
        kr_block = jnp.dot(place_ref[...], kr_t.astype(BF16), preferred_element_type=F32).T
        keys = jnp.concatenate([ckv.astype(BF16), kr_block.astype(BF16)], axis=1)
        ones_rows = (lax.broadcasted_iota(jnp.int32, (16, tk), 0) == 0).astype(F32)
        vals_t = jnp.concatenate([ckv.T, ones_rows], axis=0).astype(BF16)
        n_col = cols // col_tile
        col = lambda c: slice(c * col_tile, (c + 1) * col_tile)
        scores = lambda c: lax.dot_general(keys, qa_sc[col(c), :], (((1,), (1,)), ((), ())),
                                           preferred_element_type=F32)
        issued = {c: scores(c) for c in range(min(2, n_col))}
        for c in range(n_col):
            cs = col(c)
            st = issued.pop(c)
            if c + 2 < n_col:
                issued[c + 2] = scores(c + 2)
            m_prev = m_sc[:, cs]
            m_new = jnp.maximum(m_prev, jnp.max(st, axis=0, keepdims=True))
            alpha = jnp.exp2(m_prev - m_new)
            pt = jnp.exp2(st - m_new).astype(BF16)
            pv = jnp.dot(vals_t, pt, preferred_element_type=F32)
            l_sc[:, cs] = alpha * l_sc[:, cs] + pv[KV_LORA_RANK:KV_LORA_RANK + 1, :]
            m_sc[:, cs] = m_new
            acc_sc[:, cs] = acc_sc[:, cs] * alpha + pv[0:KV_LORA_RANK, :]

    attend(ckv_ref[0], kr_ref[0])

    @pl.when(j == pl.num_programs(1) - 1)
    def _():
        attend(ckvn_ref[0], krn_ref[0])
        o_lat = (acc_sc[...] / l_sc[...]).T.astype(BF16)
        for p in range(N_PAIRS):
            pair = sum(jnp.dot(o_lat[h * t:(h + 1) * t, :], wvp_ref[h],
                               preferred_element_type=F32) for h in (2 * p, 2 * p + 1))
            o_ref[0, p] = pair.astype(o_ref.dtype)


def _latent_attention(q, cache_ckv, cache_krt, ckv_new, krt_new, place, wkt, wvp, tile):
    b, nh, t, _ = q.shape
    past = cache_ckv.shape[1]
    cols = nh * t
    assert past % tile == 0 and cols % (2 * LANES) == 0
    return pl.pallas_call(
        _latent_attn_kernel,
        grid=(b, past // tile),
        in_specs=[
            pl.BlockSpec((1, nh, t, HEAD_BLOCK), lambda i, j: (i, 0, 0, 0)),
            pl.BlockSpec((1, tile, KV_LORA_RANK), lambda i, j: (i, j, 0)),
            pl.BlockSpec((1, QK_ROPE_DIM, tile), lambda i, j: (i, 0, j)),
            pl.BlockSpec((1, t, KV_LORA_RANK), lambda i, j: (i, 0, 0)),
            pl.BlockSpec((1, QK_ROPE_DIM, t), lambda i, j: (i, 0, 0)),
            _const_spec(place.shape), _const_spec(wkt.shape), _const_spec(wvp.shape),
        ],
        out_specs=pl.BlockSpec((1, N_PAIRS, t, LANES), lambda i, j: (i, 0, 0, 0)),
        out_shape=jax.ShapeDtypeStruct((b, N_PAIRS, t, LANES), BF16),
        scratch_shapes=[pltpu.VMEM((cols, 2 * LANES), BF16),
                        pltpu.VMEM((1, cols), F32),
                        pltpu.VMEM((1, cols), F32),
                        pltpu.VMEM((KV_LORA_RANK, cols), F32)],
        compiler_params=pltpu.CompilerParams(
            dimension_semantics=("arbitrary", "arbitrary"),
            vmem_limit_bytes=VMEM_LIMIT),
        name="latent_attention",
    )(q, cache_ckv, cache_krt, ckv_new, krt_new, place, wkt, wvp)


def _attn_kernel(q_ref, k_ref, vt_ref, o_ref, m_sc, l_sc, acc_sc, s_sc, mx_sc):
    n_group, tq = q_ref.shape[1], q_ref.shape[2]
    tk = vt_ref.shape[4]
    n_full = pl.program_id(2)

    m_sc[...] = jnp.full(m_sc.shape, NEG_INF, F32)
    l_sc[...] = jnp.zeros(l_sc.shape, F32)
    acc_sc[...] = jnp.zeros(acc_sc.shape, F32)

    def last_tile_mask():
        key = lax.broadcasted_iota(jnp.int32, (tk, tq), 0)
        qry = lax.broadcasted_iota(jnp.int32, (tk, tq), 1)
        return (key // CHUNK) <= (qry // CHUNK)

    def issue_scores(slot, h, j):
        k = k_ref[0, h, pl.ds(pl.multiple_of(j * tk, tk), tk), :]
        st = lax.dot_general(k, q_ref[0, h], (((1,), (1,)), ((), ())),
                             preferred_element_type=F32)
        s_sc[slot, h] = st
        mx_sc[slot, h] = jnp.max(st, axis=0, keepdims=True)

    def tile(j, slot, mask):
        for h in range(n_group):
            pr, lo = h // 2, (h % 2) * V_HEAD_DIM
            if mask is None:
                issue_scores(1 - slot, h, j + 1)
                st = s_sc[slot, h]
                col_max = mx_sc[slot, h]
            else:
                st = jnp.where(mask, s_sc[slot, h], NEG_INF)
                col_max = jnp.max(st, axis=0, keepdims=True)
            m_prev = m_sc[h]
            m_new = jnp.maximum(m_prev, col_max)
            alpha = jnp.exp2(m_prev - m_new)
            pt = jnp.exp2(st - m_new).astype(BF16)
            pv = jnp.dot(vt_ref[0, h, j], pt, preferred_element_type=F32)
            l_sc[h] = alpha * l_sc[h] + pv[V_HEAD_DIM:V_HEAD_DIM + 1, :]
            m_sc[h] = m_new
            acc_sc[pr, lo:lo + V_HEAD_DIM, :] = (acc_sc[pr, lo:lo + V_HEAD_DIM, :] * alpha
                                                 + pv[0:V_HEAD_DIM, :])

    def body2(jj, carry):
        tile(2 * jj, 0, None)
        tile(2 * jj + 1, 1, None)
        return carry

    def body1(j, carry):
        tile(j, 0, None)
        return carry

    for h in range(n_group):
        issue_scores(0, h, 0)
    n_pairs = n_full // 2
    lax.fori_loop(0, n_pairs, body2, 0)
    lax.fori_loop(2 * n_pairs, n_full, body1, 0)
    tile(n_full, n_full % 2, last_tile_mask())

    for pr in range(n_group // 2):
        inv_l = jnp.concatenate(
            [jnp.broadcast_to(1.0 / l_sc[2 * pr + hh], (V_HEAD_DIM, tq)) for hh in range(2)],
            axis=0)
        o_ref[0, pr] = (acc_sc[pr] * inv_l).T.astype(o_ref.dtype)


def _attention(q, k, vt, group):
    b, nh, sq, _ = q.shape
    skv = k.shape[2]
    n_kv, tk = vt.shape[2], vt.shape[4]
    tq = tk
    assert n_kv * tk == skv == sq
    gp = group // 2
    return pl.pallas_call(
        _attn_kernel,
        grid=(b, nh // group, sq // tq),
        in_specs=[
            pl.BlockSpec((1, group, tq, HEAD_BLOCK), lambda i, g, j: (i, g, j, 0)),
            pl.BlockSpec((1, group, skv, HEAD_BLOCK), lambda i, g, j: (i, g, 0, 0)),
            pl.BlockSpec((1, group, n_kv, VT_ROWS, tk), lambda i, g, j: (i, g, 0, 0, 0)),
        ],
        out_specs=pl.BlockSpec((1, gp, tq, LANES), lambda i, g, j: (i, g, j, 0)),
        out_shape=jax.ShapeDtypeStruct((b, nh // 2, sq, LANES), BF16),
        scratch_shapes=[pltpu.VMEM((group, 1, tq), F32),
                        pltpu.VMEM((group, 1, tq), F32),
                        pltpu.VMEM((gp, LANES, tq), F32),
                        pltpu.VMEM((2, group, tk, tq), F32),
                        pltpu.VMEM((2, group, 1, tq), F32)],
        compiler_params=pltpu.CompilerParams(
            dimension_semantics=("arbitrary", "arbitrary", "arbitrary"),
            vmem_limit_bytes=VMEM_LIMIT),
        name="attention",
    )(q, k, vt)


def _out_kernel(o_ref, zg_ref, x_ref, w_ref, g_ref, y_ref):
    o = jnp.concatenate([o_ref[0, p] for p in range(N_PAIRS)], axis=1)
    gated = (zg_ref[0].astype(F32) * o.astype(F32)).astype(BF16)
    x2 = x_ref[0] + jnp.dot(gated, w_ref[...], preferred_element_type=F32)
    y_ref[0] = _rms(x2, g_ref[...])


def _out_layer(o, zg, x, w_out, g, tile):
    b, s, d = x.shape
    n = s // tile
    tok = lambda w: pl.BlockSpec((1, tile, w), lambda i, j: (i, j, 0))
    return pl.pallas_call(
        _out_kernel,
        grid=(b, n),
        in_specs=[pl.BlockSpec((1, N_PAIRS, tile, LANES), lambda i, j: (i, 0, j, 0)),
                  tok(MLA_GATE_DIM), tok(d), _const_spec(w_out.shape), _const_spec((1, d))],
        out_specs=tok(d),
        out_shape=jax.ShapeDtypeStruct((b, s, d), F32),
        compiler_params=pltpu.CompilerParams(
            dimension_semantics=("arbitrary", "arbitrary"),
            vmem_limit_bytes=VMEM_LIMIT),
        name="out_layer",
    )(o, zg, x, w_out, g)


def _rope_block_cols(w):
    half = QK_ROPE_DIM // 2
    fill = jnp.zeros(w.shape[:-1] + (HEAD_BLOCK - ROPE_HI - half,), w.dtype)
    return jnp.concatenate([w, w[..., :half], fill], axis=-1)


def _prep_weights(mla_w_in, mla_w_qb, mla_w_kvb):
    o_kv = Q_LORA_RANK
    o_kr = o_kv + KV_LORA_RANK
    o_z = o_kr + QK_ROPE_DIM
    d = mla_w_in.shape[0]
    kr_cols = jnp.concatenate(
        [jnp.zeros((d, ROPE_LO), F32), _rope_block_cols(mla_w_in[:, o_kr:o_z])], axis=1)
    w_in2 = jnp.concatenate([mla_w_in[:, :o_kr], kr_cols, mla_w_in[:, o_z:]], axis=1)

    wqb = mla_w_qb.reshape(Q_LORA_RANK, N_HEADS, QK_DIM)
    wqb = jnp.concatenate([wqb[..., :QK_NOPE_DIM], _rope_block_cols(wqb[..., QK_NOPE_DIM:])], axis=-1)
    wqb = wqb.reshape(Q_LORA_RANK, N_HEADS * HEAD_BLOCK)

    wkvb = mla_w_kvb.reshape(KV_LORA_RANK, N_HEADS, QK_NOPE_DIM + V_HEAD_DIM)
    wk_pad = jnp.pad(wkvb[:, :, :QK_NOPE_DIM], ((0, 0), (0, 0), (0, HEAD_BLOCK - QK_NOPE_DIM)))
    place = np.zeros((HEAD_BLOCK, N_HEADS, HEAD_BLOCK), np.float32)
    lanes = np.arange(ROPE_LO, ROPE_HI)
    place[lanes, :, lanes] = 1.0
    wk = jnp.concatenate([wk_pad, place], axis=0).reshape(KV_LORA_RANK + HEAD_BLOCK,
                                                          N_HEADS * HEAD_BLOCK)
    wv = wkvb[:, :, QK_NOPE_DIM:]
    wvt = jnp.pad(jnp.transpose(wv, (1, 2, 0)), ((0, 0), (0, VT_ROWS - V_HEAD_DIM), (0, 0)))
    wvt = wvt.reshape(N_HEADS * VT_ROWS, KV_LORA_RANK)

    wkt = jnp.transpose(wk_pad, (1, 2, 0))
    wv_lo = jnp.pad(wv, ((0, 0), (0, 0), (0, LANES - V_HEAD_DIM)))
    wv_hi = jnp.pad(wv, ((0, 0), (0, 0), (LANES - V_HEAD_DIM, 0)))
    odd = (jnp.arange(N_HEADS) % 2 == 1)[None, :, None]
    wvp = jnp.transpose(jnp.where(odd, wv_hi, wv_lo), (1, 0, 2))
    return tuple(w.astype(BF16) for w in (w_in2, wqb, wk, wvt, wkt, wvp))


def _inv_freq_lanes():
    inv = 1.0 / (ROPE_THETA ** (jnp.arange(0, QK_ROPE_DIM, 2, dtype=F32) / QK_ROPE_DIM))
    z = lambda n: jnp.zeros((n,), F32)
    return jnp.concatenate([z(ROPE_LO), inv, inv, z(HEAD_BLOCK - ROPE_HI)])[None, :]


def _rope_placement():
    place = np.zeros((HEAD_BLOCK, QK_ROPE_DIM), np.float32)
    rows = np.arange(QK_ROPE_DIM)
    place[ROPE_LO + rows, rows] = 1.0
    return jnp.asarray(place, BF16)


def kernel(x_prompt, x_sample, state_conv, cache_ckv, cache_krope, norm_g, final_norm_g,
           conv_w_in, conv_w, conv_w_out, mla_w_in, mla_q_norm_g, mla_w_qb,
           mla_kv_norm_g, mla_w_kvb, mla_w_out):
    bp, s, d = x_prompt.shape
    bs, t, _ = x_sample.shape
    past = cache_ckv.shape[2]

    tok_tile = 512
    attn_tile = 512
    cache_tile = 1024
    group = 4

    g0 = norm_g[0][None, :]
    g1 = norm_g[1][None, :]
    gf = final_norm_g[None, :]
    cw_in = conv_w_in[0].astype(BF16)
    cw_out = conv_w_out[0].astype(BF16)
    w_in2, wqb, wk, wvt, wkt, wvp = _prep_weights(mla_w_in[0], mla_w_qb[0], mla_w_kvb[0])
    qg = mla_q_norm_g[0][None, :]
    kvg = mla_kv_norm_g[0][None, :]
    w_out = mla_w_out[0].astype(BF16)
    inv_lane = _inv_freq_lanes()

    zeros_prev = jnp.zeros((bp, CONV_WIDTH - 1, d), F32)
    xp1, st_p = _conv_layer(x_prompt, zeros_prev, g0, cw_in, conv_w[0], cw_out, tok_tile)
    xs1, st_s = _conv_layer(x_sample, state_conv[0], g0, cw_in, conv_w[0], cw_out, t)

    cos_p, sin_p = _rope_tables(inv_lane, s, 0, 512)
    cos_s, sin_s = _rope_tables(inv_lane, t, past, t)

    qp, zg_p, ckv_p, krt_p, kp, vp = _mla_project(
        xp1, g1, w_in2, qg, wqb, kvg, wk, wvt, cos_p, sin_p, attn_tile, True)
    qs, zg_s, ckv_s, krt_s = _mla_project(
        xs1, g1, w_in2, qg, wqb, kvg, wk, wvt, cos_s, sin_s, t, False)

    op = _attention(qp, kp, vp, group)
    cache_krt = jnp.swapaxes(cache_krope[0], 1, 2)
    os_ = _latent_attention(qs, cache_ckv[0], cache_krt, ckv_s, krt_s,
                            _rope_placement(), wkt, wvp, cache_tile)

    y_prompt = _out_layer(op, zg_p, xp1, w_out, gf, tok_tile)
    y_sample = _out_layer(os_, zg_s, xs1, w_out, gf, t)

    return (y_prompt, y_sample, st_p[None], ckv_p[None], jnp.swapaxes(krt_p, 1, 2)[None],
            st_s[None], ckv_s[None], jnp.swapaxes(krt_s, 1, 2)[None])
```

```python
import functools
import math

import jax
import jax.numpy as jnp
import numpy as np
from jax import lax
from jax.experimental import pallas as pl
from jax.experimental.pallas import tpu as pltpu

D_MODEL = 1024
CHUNK = 64
CONV_WIDTH = 3
N_HEADS = 16
QK_NOPE_DIM = 64
QK_ROPE_DIM = 32
V_HEAD_DIM = 64
QK_DIM = QK_NOPE_DIM + QK_ROPE_DIM
Q_LORA_RANK = 256
KV_LORA_RANK = 128
MLA_GATE_DIM = N_HEADS * V_HEAD_DIM
ROPE_THETA = 10000.0
EPS = 1e-6
NEG_INF = -1e30
PAST_LEN = 4096

LANES = 128
HEAD_BLOCK = LANES
N_PAIRS = N_HEADS // 2
VT_ROWS = V_HEAD_DIM + 16
ROPE_LO = QK_NOPE_DIM
ROPE_MID = ROPE_LO + QK_ROPE_DIM // 2
ROPE_HI = ROPE_LO + QK_ROPE_DIM
Q_SCALE = (QK_DIM ** -0.5) * math.log2(math.e)

VMEM_LIMIT =56 * 1024 * 1024
F32 = jnp.float32
BF16 = jnp.bfloat16


def _rms(x, g):
    ms = jnp.mean(x * x, axis=-1, keepdims=True)
    return x * lax.rsqrt(ms + EPS) * g


def _const_spec(shape):
    nd = len(shape)
    return pl.BlockSpec(shape, lambda *_: (0,) * nd)


def _conv_kernel(x_ref, prev_ref, g_ref, win_ref, cw_ref, wout_ref, xo_ref, st_ref, ubuf):
    j = pl.program_id(1)
    t = x_ref.shape[1]
    halo = CONV_WIDTH - 1
    base = 8

    @pl.when(j == 0)
    def _():
        ubuf[base - halo:base, :] = prev_ref[0]

    x = x_ref[0]
    hb = _rms(x, g_ref[...]).astype(BF16)

    def proj(k):
        return jnp.dot(hb, win_ref[:, k * D_MODEL:(k + 1) * D_MODEL],
                       preferred_element_type=F32)

    u = proj(1) * proj(2)
    ubuf[base:base + t, :] = u
    conv = (ubuf[base - 2:base - 2 + t, :] * cw_ref[0:1, :]
            + ubuf[base - 1:base - 1 + t, :] * cw_ref[1:2, :]
            + u * cw_ref[2:3, :])
    new_halo = ubuf[base + t - halo:base + t, :]
    ubuf[base - halo:base, :] = new_halo
    z = proj(3)
    gated = (z * jax.nn.sigmoid(z)) * proj(0) * conv
    y = jnp.dot(gated.astype(BF16), wout_ref[...], preferred_element_type=F32)
    xo_ref[0] = x + y

    @pl.when(j == pl.num_programs(1) - 1)
    def _():
        st_ref[0] = new_halo


def _conv_layer(x, prev, g, w_in, conv_w, w_out, tile):
    b, s, d = x.shape
    n = s // tile
    return pl.pallas_call(
        _conv_kernel,
        grid=(b, n),
        in_specs=[
            pl.BlockSpec((1, tile, d), lambda i, j: (i, j, 0)),
            pl.BlockSpec((1, CONV_WIDTH - 1, d), lambda i, j: (i, 0, 0)),
            _const_spec((1, d)),
            _const_spec(w_in.shape),
            _const_spec(conv_w.shape),
            _const_spec(w_out.shape),
        ],
        out_specs=[
            pl.BlockSpec((1, tile, d), lambda i, j: (i, j, 0)),
            pl.BlockSpec((1, CONV_WIDTH - 1, d), lambda i, j: (i, 0, 0)),
        ],
        out_shape=[
            jax.ShapeDtypeStruct((b, s, d), F32),
            jax.ShapeDtypeStruct((b, CONV_WIDTH - 1, d), F32),
        ],
        scratch_shapes=[pltpu.VMEM((tile + 8, d), F32)],
        compiler_params=pltpu.CompilerParams(
            dimension_semantics=("arbitrary", "arbitrary"),
            vmem_limit_bytes=VMEM_LIMIT),
        name="conv_layer",
    )(x, prev, g, w_in, conv_w, w_out)


def _rope_table_kernel(inv_ref, cos_ref, sin_ref, *, offset):
    t = cos_ref.shape[0]
    row = lax.broadcasted_iota(jnp.int32, (t, LANES), 0)
    lane = lax.broadcasted_iota(jnp.int32, (t, LANES), 1)
    pos = (offset + pl.program_id(0) * t + row).astype(F32)
    ang = pos * inv_ref[...]
    cos_ref[...] = jnp.cos(ang)
    s = jnp.sin(ang)
    sin_ref[...] = jnp.where(lane < ROPE_MID, -s, s)


def _rope_tables(inv_lane, s, offset, tile):
    return pl.pallas_call(
        functools.partial(_rope_table_kernel, offset=offset),
        grid=(s // tile,),
        in_specs=[_const_spec((1, LANES))],
        out_specs=[pl.BlockSpec((tile, LANES), lambda i: (i, 0))] * 2,
        out_shape=[jax.ShapeDtypeStruct((s, LANES), F32)] * 2,
        compiler_params=pltpu.CompilerParams(dimension_semantics=("arbitrary",)),
        name="rope_tables",
    )(inv_lane)


def _rope(v, cos, sin_signed):
    partner = pltpu.roll(v, LANES - QK_ROPE_DIM // 2, 1)
    return v * cos + partner * sin_signed


def _expand_kv(ckv, kr_block, wk_ref, wvt_ref, k_ref, vt_ref):
    t = ckv.shape[0]
    k_in = jnp.concatenate([ckv.astype(BF16), kr_block.astype(BF16)], axis=1)
    k_all = jnp.dot(k_in, wk_ref[...], preferred_element_type=F32)
    for h in range(N_HEADS):
        k_ref[0, h] = k_all[:, h * HEAD_BLOCK:(h + 1) * HEAD_BLOCK].astype(BF16)
    vt = jnp.dot(wvt_ref[...], ckv.T.astype(BF16), preferred_element_type=F32)
    ones_row = (lax.broadcasted_iota(jnp.int32, (VT_ROWS, t), 0) == V_HEAD_DIM).astype(F32)
    for h in range(N_HEADS):
        vt_ref[0, h, 0] = (vt[h * VT_ROWS:(h + 1) * VT_ROWS, :] + ones_row).astype(BF16)


def _vt_spec(tile):
    return pl.BlockSpec((1, N_HEADS, 1, VT_ROWS, tile), lambda i, j: (i, 0, j, 0, 0))


def _mla_project_kernel(x_ref, g_ref, win_ref, qg_ref, wqb_ref, kvg_ref, wk_ref, wvt_ref,
                        cos_ref, sin_ref, q_ref, ckv_ref, kr_ref, *kv_refs):
    hb = _rms(x_ref[0], g_ref[...]).astype(BF16)
    o_kv = Q_LORA_RANK
    o_kr = o_kv + KV_LORA_RANK
    o_z = o_kr + HEAD_BLOCK

    def proj(lo, hi):
        return jnp.dot(hb, win_ref[:, lo:hi], preferred_element_type=F32)

    ckv = _rms(proj(o_kv, o_kr), kvg_ref[...])
    ckv_ref[0] = ckv

    cos = cos_ref[...]
    sin_signed = sin_ref[...]
    kr_block = _rope(proj(o_kr, o_z), cos, sin_signed)
    kr_ref[0] = kr_block.T[ROPE_LO:ROPE_HI, :]

    qn = _rms(proj(0, o_kv), qg_ref[...]).astype(BF16)
    q = jnp.dot(qn, wqb_ref[...], preferred_element_type=F32)
    cos_q = cos * Q_SCALE
    sin_q = sin_signed * Q_SCALE
    for h in range(N_HEADS):
        qh = _rope(q[:, h * HEAD_BLOCK:(h + 1) * HEAD_BLOCK], cos_q, sin_q)
        q_ref[0, h] = qh.astype(BF16)

    if kv_refs:
        _expand_kv(ckv, kr_block, wk_ref, wvt_ref, *kv_refs)


def _mla_project(x, g, w_in2, qg, wqb, kvg, wk, wvt, cos, sin_signed, tile, emit_kv):
    b, s, d = x.shape
    n = s // tile
    tok = lambda w: pl.BlockSpec((1, tile, w), lambda i, j: (i, j, 0))
    head = lambda nh: pl.BlockSpec((1, nh, tile, LANES), lambda i, j: (i, 0, j, 0))
    out_specs = [head(N_HEADS), tok(KV_LORA_RANK),
                 pl.BlockSpec((1, QK_ROPE_DIM, tile), lambda i, j: (i, 0, j))]
    out_shape = [
        jax.ShapeDtypeStruct((b, N_HEADS, s, HEAD_BLOCK), BF16),
        jax.ShapeDtypeStruct((b, s, KV_LORA_RANK), F32),
        jax.ShapeDtypeStruct((b, QK_ROPE_DIM, s), F32),
    ]
    if emit_kv:
        out_specs += [head(N_HEADS), _vt_spec(tile)]
        out_shape += [jax.ShapeDtypeStruct((b, N_HEADS, s, HEAD_BLOCK), BF16),
                      jax.ShapeDtypeStruct((b, N_HEADS, n, VT_ROWS, tile), BF16)]
    return pl.pallas_call(
        _mla_project_kernel,
        grid=(b, n),
        in_specs=[
            tok(d),
            _const_spec((1, d)),
            _const_spec(w_in2.shape),
            _const_spec(qg.shape),
            _const_spec(wqb.shape),
            _const_spec(kvg.shape),
            _const_spec(wk.shape),
            _const_spec(wvt.shape),
            pl.BlockSpec((tile, LANES), lambda i, j: (j, 0)),
            pl.BlockSpec((tile, LANES), lambda i, j: (j, 0)),
        ],
        out_specs=out_specs,
        out_shape=out_shape,
        compiler_params=pltpu.CompilerParams(
            dimension_semantics=("arbitrary", "arbitrary"),
            vmem_limit_bytes=VMEM_LIMIT),
        name="mla_project",
    )(x, g, w_in2, qg, wqb, kvg, wk, wvt, cos, sin_signed)


def _latent_attn_kernel(q_ref, ckv_ref, kr_ref, ckvn_ref, krn_ref, place_ref, wkt_ref, wvp_ref,
                        o_ref, qa_sc, m_sc, l_sc, acc_sc):
    j = pl.program_id(1)
    t = q_ref.shape[2]
    cols = N_HEADS * t
    col_tile = 2 * LANES

    @pl.when(j == 0)
    def _():
        lane = lax.broadcasted_iota(jnp.int32, (t, LANES), 1)
        is_rope = (lane >= ROPE_LO) & (lane < ROPE_HI)
        for h in range(N_HEADS):
            qh = q_ref[0, h]
            q_lat = jnp.dot(qh, wkt_ref[h], preferred_element_type=F32)
            qa_sc[h * t:(h + 1) * t, 0:LANES] = q_lat.astype(BF16)
            qa_sc[h * t:(h + 1) * t, LANES:2 * LANES] = jnp.where(is_rope, qh, jnp.zeros_like(qh))
        m_sc[...] = jnp.full(m_sc.shape, NEG_INF, F32)
        l_sc[...] = jnp.zeros(l_sc.shape, F32)
        acc_sc[...] = jnp.zeros(acc_sc.shape, F32)

    def attend(ckv, kr_t):
        tk = ckv.shape[0]
        kr_block = jnp.dot(place_ref[...], kr_t.astype(BF16), preferred_element_type=F32).T
        keys = jnp.concatenate([ckv.astype(BF16), kr_block.astype(BF16)], axis=1)
        ones_rows = (lax.broadcasted_iota(jnp.int32, (16, tk), 0) == 0).astype(F32)
        vals_t = jnp.concatenate([ckv.T, ones_rows], axis=0).astype(BF16)
        n_col = cols // col_tile
        col = lambda c: slice(c * col_tile, (c + 1) * col_tile)
        scores = lambda c: lax.dot_general(keys, qa_sc[col(c), :], (((1,), (1,)), ((), ())),
                                           preferred_element_type=F32)
        issued = {c: scores(c) for c in range(min(2, n_col))}
        for c in range(n_col):
            cs = col(c)
            st = issued.pop(c)
            if c + 2 < n_col:
                issued[c + 2] = scores(c + 2)
            m_prev = m_sc[:, cs]
            m_new = jnp.maximum(m_prev, jnp.max(st, axis=0, keepdims=True))
            alpha = jnp.exp2(m_prev - m_new)
            pt = jnp.exp2(st - m_new).astype(BF16)
            pv = jnp.dot(vals_t, pt, preferred_element_type=F32)
            l_sc[:, cs] = alpha * l_sc[:, cs] + pv[KV_LORA_RANK:KV_LORA_RANK + 1, :]
            m_sc[:, cs] = m_new
            acc_sc[:, cs] = acc_sc[:, cs] * alpha + pv[0:KV_LORA_RANK, :]

    attend(ckv_ref[0], kr_ref[0])

    @pl.when(j == pl.num_programs(1) - 1)
    def _():
        attend(ckvn_ref[0], krn_ref[0])
        o_lat = (acc_sc[...] / l_sc[...]).T.astype(BF16)
        for p in range(N_PAIRS):
            pair = sum(jnp.dot(o_lat[h * t:(h + 1) * t, :], wvp_ref[h],
                               preferred_element_type=F32) for h in (2 * p, 2 * p + 1))
            o_ref[0, p] = pair.astype(o_ref.dtype)


def _latent_attention(q, cache_ckv, cache_krt, ckv_new, krt_new, place, wkt, wvp, tile):
    b, nh, t, _ = q.shape
    past = cache_ckv.shape[1]
    cols = nh * t
    assert past % tile == 0 and cols % (2 * LANES) == 0
    return pl.pallas_call(
        _latent_attn_kernel,
        grid=(b, past // tile),
        in_specs=[
            pl.BlockSpec((1, nh, t, HEAD_BLOCK), lambda i, j: (i, 0, 0, 0)),
            pl.BlockSpec((1, tile, KV_LORA_RANK), lambda i, j: (i, j, 0)),
            pl.BlockSpec((1, QK_ROPE_DIM, tile), lambda i, j: (i, 0, j)),
            pl.BlockSpec((1, t, KV_LORA_RANK), lambda i, j: (i, 0, 0)),
            pl.BlockSpec((1, QK_ROPE_DIM, t), lambda i, j: (i, 0, 0)),
            _const_spec(place.shape), _const_spec(wkt.shape), _const_spec(wvp.shape),
        ],
        out_specs=pl.BlockSpec((1, N_PAIRS, t, LANES), lambda i, j: (i, 0, 0, 0)),
        out_shape=jax.ShapeDtypeStruct((b, N_PAIRS, t, LANES), BF16),
        scratch_shapes=[pltpu.VMEM((cols, 2 * LANES), BF16),
                        pltpu.VMEM((1, cols), F32),
                        pltpu.VMEM((1, cols), F32),
                        pltpu.VMEM((KV_LORA_RANK, cols), F32)],
        compiler_params=pltpu.CompilerParams(
            dimension_semantics=("arbitrary", "arbitrary"),
            vmem_limit_bytes=VMEM_LIMIT),
        name="latent_attention",
    )(q, cache_ckv, cache_krt, ckv_new, krt_new, place, wkt, wvp)


def _attn_kernel(q_ref, qn_ref, k_ref, vt_ref, o_ref, m_sc, l_sc, acc_sc, s_sc, mx_sc, d_sc):
    n_group, tq = q_ref.shape[1], q_ref.shape[2]
    tk = vt_ref.shape[4]
    i = pl.program_id(2)
    last_q = pl.num_programs(2) - 1

    m_sc[...] = jnp.full(m_sc.shape, NEG_INF, F32)
    l_sc[...] = jnp.zeros(l_sc.shape, F32)
    acc_sc[...] = jnp.zeros(acc_sc.shape, F32)

    def scores(qr, h, j):
        k = k_ref[0, h, pl.ds(pl.multiple_of(j * tk, tk), tk), :]
        return lax.dot_general(k, qr[0, h], (((1,), (1,)), ((), ())),
                               preferred_element_type=F32)

    def issue_scores(slot, h, j):
        st = scores(q_ref, h, j)
        s_sc[slot, h] = st
        mx_sc[slot, h] = jnp.max(st, axis=0, keepdims=True)

    def issue_next_diagonal(h):
        d_sc[h] = scores(qn_ref, h, jnp.minimum(i + 1, last_q))

    def pv_dot(vt, st, m):
        return jnp.dot(vt, jnp.exp2(st - m).astype(BF16), preferred_element_type=F32)

    def online_update(h, col_max, pv_of):
        pr, lo = h // 2, (h % 2) * V_HEAD_DIM
        m_prev = m_sc[h]
        m_new = jnp.maximum(m_prev, col_max)
        alpha = jnp.exp2(m_prev - m_new)
        pv = pv_of(m_new)
        l_sc[h] = alpha * l_sc[h] + pv[V_HEAD_DIM:V_HEAD_DIM + 1, :]
        m_sc[h] = m_new
        acc_sc[pr, lo:lo + V_HEAD_DIM, :] = (acc_sc[pr, lo:lo + V_HEAD_DIM, :] * alpha
                                             + pv[0:V_HEAD_DIM, :])

    def tile(j, slot, issue):
        for h in range(n_group):
            issue(h)
            online_update(h, mx_sc[slot, h],
                          lambda m, h=h: pv_dot(vt_ref[0, h, j], s_sc[slot, h], m))

    hk, hq = tk // 2, tq // 2
    quadrant_mask = (lax.broadcasted_iota(jnp.int32, (hk, hq), 0) // CHUNK
                     <= lax.broadcasted_iota(jnp.int32, (hk, hq), 1) // CHUNK)

    def diagonal_tile(h):
        mask = quadrant_mask
        st_aa = jnp.where(mask, d_sc[h, 0:hk, 0:hq], NEG_INF)
        st_ab = d_sc[h, 0:hk, hq:tq]
        st_bb = jnp.where(mask, d_sc[h, hk:tk, hq:tq], NEG_INF)
        col_max = jnp.concatenate(
            [jnp.max(st_aa, axis=0, keepdims=True),
             jnp.maximum(jnp.max(st_ab, axis=0, keepdims=True),
                         jnp.max(st_bb, axis=0, keepdims=True))], axis=1)
        vt_a = vt_ref[0, h, i, :, 0:hk]
        vt_b = vt_ref[0, h, i, :, hk:tk]
        online_update(h, col_max, lambda m: jnp.concatenate(
            [pv_dot(vt_a, st_aa, m[:, 0:hq]),
             pv_dot(vt_a, st_ab, m[:, hq:tq]) + pv_dot(vt_b, st_bb, m[:, hq:tq])], axis=1))

    @pl.when(i == 0)
    def _():
        for h in range(n_group):
            d_sc[h] = scores(q_ref, h, 0)

    for h in range(n_group):
        issue_scores(0, h, 0)
        diagonal_tile(h)

    def body2(jj, carry):
        tile(2 * jj, 0, lambda h: issue_scores(1, h, 2 * jj + 1))
        tile(2 * jj + 1, 1, lambda h: issue_scores(0, h, 2 * jj + 2))
        return carry

    def body1(j, carry):
        tile(j, 0, lambda h: issue_scores(1, h, j + 1))
        return carry

    n_loop = jnp.maximum(i - 1, 0)
    n_pairs = n_loop // 2
    lax.fori_loop(0, n_pairs, body2, 0)
    lax.fori_loop(2 * n_pairs, n_loop, body1, 0)

    @pl.when(i > 0)
    def _():
        tile(i - 1, (i - 1) % 2, issue_next_diagonal)

    @pl.when(i == 0)
    def _():
        for h in range(n_group):
            issue_next_diagonal(h)

    for pr in range(n_group // 2):
        inv_l = jnp.concatenate(
            [jnp.broadcast_to(1.0 / l_sc[2 * pr + hh], (V_HEAD_DIM, tq)) for hh in range(2)],
            axis=0)
        o_ref[0, pr] = (acc_sc[pr] * inv_l).T.astype(o_ref.dtype)


def _attention(q, k, vt, group):
    b, nh, sq, _ = q.shape
    skv = k.shape[2]
    n_kv, tk = vt.shape[2], vt.shape[4]
    tq = tk
    assert n_kv * tk == skv == sq and (tk // 2) % CHUNK == 0
    gp = group // 2
    n_q = sq // tq
    return pl.pallas_call(
        _attn_kernel,
        grid=(b, nh // group, n_q),
        in_specs=[
            pl.BlockSpec((1, group, tq, HEAD_BLOCK), lambda i, g, j: (i, g, j, 0)),
            pl.BlockSpec((1, group, tq, HEAD_BLOCK),
                         lambda i, g, j: (i, g, jnp.minimum(j + 1, n_q - 1), 0)),
            pl.BlockSpec((1, group, skv, HEAD_BLOCK), lambda i, g, j: (i, g, 0, 0)),
            pl.BlockSpec((1, group, n_kv, VT_ROWS, tk), lambda i, g, j: (i, g, 0, 0, 0)),
        ],
        out_specs=pl.BlockSpec((1, gp, tq, LANES), lambda i, g, j: (i, g, j, 0)),
        out_shape=jax.ShapeDtypeStruct((b, nh // 2, sq, LANES), BF16),
        scratch_shapes=[pltpu.VMEM((group, 1, tq), F32),
                        pltpu.VMEM((group, 1, tq), F32),
                        pltpu.VMEM((gp, LANES, tq), F32),
                        pltpu.VMEM((2, group, tk, tq), F32),
                        pltpu.VMEM((2, group, 1, tq), F32),
                        pltpu.VMEM((group, tk, tq), F32)],
        compiler_params=pltpu.CompilerParams(
            dimension_semantics=("arbitrary", "arbitrary", "arbitrary"),
            vmem_limit_bytes=VMEM_LIMIT),
        name="attention",
    )(q, q, k, vt)


def _out_kernel(o_ref, x_ref, g1_ref, wz_ref, w_ref, g_ref, y_ref):
    x = x_ref[0]
    z = jnp.dot(_rms(x, g1_ref[...]).astype(BF16), wz_ref[...], preferred_element_type=F32)
    o = jnp.concatenate([o_ref[0, p] for p in range(N_PAIRS)], axis=1)
    gated = (z * jax.nn.sigmoid(z) * o.astype(F32)).astype(BF16)
    x2 = x + jnp.dot(gated, w_ref[...], preferred_element_type=F32)
    y_ref[0] = _rms(x2, g_ref[...])


def _out_layer(o, x, g1, w_z, w_out, g, tile):
    b, s, d = x.shape
    n = s // tile
    tok = lambda w: pl.BlockSpec((1, tile, w), lambda i, j: (i, j, 0))
    return pl.pallas_call(
        _out_kernel,
        grid=(b, n),
        in_specs=[pl.BlockSpec((1, N_PAIRS, tile, LANES), lambda i, j: (i, 0, j, 0)),
                  tok(d), _const_spec((1, d)), _const_spec(w_z.shape),
                  _const_spec(w_out.shape), _const_spec((1, d))],
        out_specs=tok(d),
        out_shape=jax.ShapeDtypeStruct((b, s, d), F32),
        compiler_params=pltpu.CompilerParams(
            dimension_semantics=("arbitrary", "arbitrary"),
            vmem_limit_bytes=VMEM_LIMIT),
        name="out_layer",
    )(o, x, g1, w_z, w_out, g)


def _rope_block_cols(w):
    half = QK_ROPE_DIM // 2
    fill = jnp.zeros(w.shape[:-1] + (HEAD_BLOCK - ROPE_HI - half,), w.dtype)
    return jnp.concatenate([w, w[..., :half], fill], axis=-1)


def _prep_weights(mla_w_in, mla_w_qb, mla_w_kvb):
    o_kv = Q_LORA_RANK
    o_kr = o_kv + KV_LORA_RANK
    o_z = o_kr + QK_ROPE_DIM
    d = mla_w_in.shape[0]
    kr_cols = jnp.concatenate(
        [jnp.zeros((d, ROPE_LO), F32), _rope_block_cols(mla_w_in[:, o_kr:o_z])], axis=1)
    w_in2 = jnp.concatenate([mla_w_in[:, :o_kr], kr_cols], axis=1)
    w_z = mla_w_in[:, o_z:]

    wqb = mla_w_qb.reshape(Q_LORA_RANK, N_HEADS, QK_DIM)
    wqb = jnp.concatenate([wqb[..., :QK_NOPE_DIM], _rope_block_cols(wqb[..., QK_NOPE_DIM:])], axis=-1)
    wqb = wqb.reshape(Q_LORA_RANK, N_HEADS * HEAD_BLOCK)

    wkvb = mla_w_kvb.reshape(KV_LORA_RANK, N_HEADS, QK_NOPE_DIM + V_HEAD_DIM)
    wk_pad = jnp.pad(wkvb[:, :, :QK_NOPE_DIM], ((0, 0), (0, 0), (0, HEAD_BLOCK - QK_NOPE_DIM)))
    place = np.zeros((HEAD_BLOCK, N_HEADS, HEAD_BLOCK), np.float32)
    lanes = np.arange(ROPE_LO, ROPE_HI)
    place[lanes, :, lanes] = 1.0
    wk = jnp.concatenate([wk_pad, place], axis=0).reshape(KV_LORA_RANK + HEAD_BLOCK,
                                                          N_HEADS * HEAD_BLOCK)
    wv = wkvb[:, :, QK_NOPE_DIM:]
    wvt = jnp.pad(jnp.transpose(wv, (1, 2, 0)), ((0, 0), (0, VT_ROWS - V_HEAD_DIM), (0, 0)))
    wvt = wvt.reshape(N_HEADS * VT_ROWS, KV_LORA_RANK)

    wkt = jnp.transpose(wk_pad, (1, 2, 0))
    wv_lo = jnp.pad(wv, ((0, 0), (0, 0), (0, LANES - V_HEAD_DIM)))
    wv_hi = jnp.pad(wv, ((0, 0), (0, 0), (LANES - V_HEAD_DIM, 0)))
    odd = (jnp.arange(N_HEADS) % 2 == 1)[None, :, None]
    wvp = jnp.transpose(jnp.where(odd, wv_hi, wv_lo), (1, 0, 2))
    return tuple(w.astype(BF16) for w in (w_in2, w_z, wqb, wk, wvt, wkt, wvp))


def _inv_freq_lanes():
    inv = 1.0 / (ROPE_THETA ** (jnp.arange(0, QK_ROPE_DIM, 2, dtype=F32) / QK_ROPE_DIM))
    z = lambda n: jnp.zeros((n,), F32)
    return jnp.concatenate([z(ROPE_LO), inv, inv, z(HEAD_BLOCK - ROPE_HI)])[None, :]


def _rope_placement():
    place = np.zeros((HEAD_BLOCK, QK_ROPE_DIM), np.float32)
    rows = np.arange(QK_ROPE_DIM)
    place[ROPE_LO + rows, rows] = 1.0
    return jnp.asarray(place, BF16)


def kernel(x_prompt, x_sample, state_conv, cache_ckv, cache_krope, norm_g, final_norm_g,
           conv_w_in, conv_w, conv_w_out, mla_w_in, mla_q_norm_g, mla_w_qb,
           mla_kv_norm_g, mla_w_kvb, mla_w_out):
    bp, s, d = x_prompt.shape
    bs, t, _ = x_sample.shape
    past = cache_ckv.shape[2]

    tok_tile = 512
    attn_tile = 512
    cache_tile = 1024
    group = 4

    g0 = norm_g[0][None, :]
    g1 = norm_g[1][None, :]
    gf = final_norm_g[None, :]
    cw_in = conv_w_in[0].astype(BF16)
    cw_out = conv_w_out[0].astype(BF16)
    w_in2, w_z, wqb, wk, wvt, wkt, wvp = _prep_weights(mla_w_in[0], mla_w_qb[0], mla_w_kvb[0])
    qg = mla_q_norm_g[0][None, :]
    kvg = mla_kv_norm_g[0][None, :]
    w_out = mla_w_out[0].astype(BF16)
    inv_lane = _inv_freq_lanes()

    zeros_prev = jnp.zeros((bp, CONV_WIDTH - 1, d), F32)
    xp1, st_p = _conv_layer(x_prompt, zeros_prev, g0, cw_in, conv_w[0], cw_out, tok_tile)
    xs1, st_s = _conv_layer(x_sample, state_conv[0], g0, cw_in, conv_w[0], cw_out, t)

    cos_p, sin_p = _rope_tables(inv_lane, s, 0, 512)
    cos_s, sin_s = _rope_tables(inv_lane, t, past, t)

    qp, ckv_p, krt_p, kp, vp = _mla_project(
        xp1, g1, w_in2, qg, wqb, kvg, wk, wvt, cos_p, sin_p, attn_tile, True)
    qs, ckv_s, krt_s = _mla_project(
        xs1, g1, w_in2, qg, wqb, kvg, wk, wvt, cos_s, sin_s, t, False)

    op = _attention(qp, kp, vp, group)
    cache_krt = jnp.swapaxes(cache_krope[0], 1, 2)
    os_ = _latent_attention(qs, cache_ckv[0], cache_krt, ckv_s, krt_s,
                            _rope_placement(), wkt, wvp, cache_tile)

    y_prompt = _out_layer(op, xp1, g1, w_z, w_out, gf, tok_tile)
    y_sample = _out_layer(os_, xs1, g1, w_z, w_out, gf, t)

    return (y_prompt, y_sample, st_p[None], ckv_p[None], jnp.swapaxes(krt_p, 1, 2)[None],
            st_s[None], ckv_s[None], jnp.swapaxes(krt_s, 1, 2)[None])
```

```python
import functools
import math

import jax
import jax.numpy as jnp
import numpy as np
from jax import lax
from jax.experimental import pallas as pl
from jax.experimental.pallas import tpu as pltpu

D_MODEL = 1024
CHUNK = 64
CONV_WIDTH = 3
N_HEADS = 16
QK_NOPE_DIM = 64
QK_ROPE_DIM = 32
V_HEAD_DIM = 64
QK_DIM = QK_NOPE_DIM + QK_ROPE_DIM
Q_LORA_RANK = 256
KV_LORA_RANK = 128
MLA_GATE_DIM = N_HEADS * V_HEAD_DIM
ROPE_THETA = 10000.0
EPS = 1e-6
NEG_INF = -1e30
PAST_LEN = 4096

LANES = 128
HEAD_BLOCK = LANES
N_PAIRS = N_HEADS // 2
VT_ROWS = V_HEAD_DIM + 16
ROPE_LO = QK_NOPE_DIM
ROPE_MID = ROPE_LO + QK_ROPE_DIM // 2
ROPE_HI = ROPE_LO + QK_ROPE_DIM
Q_SCALE = (QK_DIM ** -0.5) * math.log2(math.e)

VMEM_LIMIT =56 * 1024 * 1024
F32 = jnp.float32
BF16 = jnp.bfloat16


def _rms(x, g):
    ms = jnp.mean(x * x, axis=-1, keepdims=True)
    return x * lax.rsqrt(ms + EPS) * g


def _const_spec(shape):
    nd = len(shape)
    return pl.BlockSpec(shape, lambda *_: (0,) * nd)


def _conv_kernel(x_ref, prev_ref, g_ref, win_ref, cw_ref, wout_ref, xo_ref, st_ref, ubuf):
    j = pl.program_id(1)
    t = x_ref.shape[1]
    halo = CONV_WIDTH - 1
    base = 8

    @pl.when(j == 0)
    def _():
        ubuf[base - halo:base, :] = prev_ref[0]

    x = x_ref[0]
    hb = _rms(x, g_ref[...]).astype(BF16)

    def proj(k):
        return jnp.dot(hb, win_ref[:, k * D_MODEL:(k + 1) * D_MODEL],
                       preferred_element_type=F32)

    u = proj(1) * proj(2)
    ubuf[base:base + t, :] = u
    conv = (ubuf[base - 2:base - 2 + t, :] * cw_ref[0:1, :]
            + ubuf[base - 1:base - 1 + t, :] * cw_ref[1:2, :]
            + u * cw_ref[2:3, :])
    new_halo = ubuf[base + t - halo:base + t, :]
    ubuf[base - halo:base, :] = new_halo
    z = proj(3)
    gated = (z * jax.nn.sigmoid(z)) * proj(0) * conv
    y = jnp.dot(gated.astype(BF16), wout_ref[...], preferred_element_type=F32)
    xo_ref[0] = x + y

    @pl.when(j == pl.num_programs(1) - 1)
    def _():
        st_ref[0] = new_halo


def _conv_layer(x, prev, g, w_in, conv_w, w_out, tile):
    b, s, d = x.shape
    n = s // tile
    return pl.pallas_call(
        _conv_kernel,
        grid=(b, n),
        in_specs=[
            pl.BlockSpec((1, tile, d), lambda i, j: (i, j, 0)),
            pl.BlockSpec((1, CONV_WIDTH - 1, d), lambda i, j: (i, 0, 0)),
            _const_spec((1, d)),
            _const_spec(w_in.shape),
            _const_spec(conv_w.shape),
            _const_spec(w_out.shape),
        ],
        out_specs=[
            pl.BlockSpec((1, tile, d), lambda i, j: (i, j, 0)),
            pl.BlockSpec((1, CONV_WIDTH - 1, d), lambda i, j: (i, 0, 0)),
        ],
        out_shape=[
            jax.ShapeDtypeStruct((b, s, d), F32),
            jax.ShapeDtypeStruct((b, CONV_WIDTH - 1, d), F32),
        ],
        scratch_shapes=[pltpu.VMEM((tile + 8, d), F32)],
        compiler_params=pltpu.CompilerParams(
            dimension_semantics=("arbitrary", "arbitrary"),
            vmem_limit_bytes=VMEM_LIMIT),
        name="conv_layer",
    )(x, prev, g, w_in, conv_w, w_out)


def _rope_table_kernel(inv_ref, cos_ref, sin_ref, *, offset):
    t = cos_ref.shape[0]
    row = lax.broadcasted_iota(jnp.int32, (t, LANES), 0)
    lane = lax.broadcasted_iota(jnp.int32, (t, LANES), 1)
    pos = (offset + pl.program_id(0) * t + row).astype(F32)
    ang = pos * inv_ref[...]
    cos_ref[...] = jnp.cos(ang)
    s = jnp.sin(ang)
    sin_ref[...] = jnp.where(lane < ROPE_MID, -s, s)


def _rope_tables(inv_lane, s, offset, tile):
    return pl.pallas_call(
        functools.partial(_rope_table_kernel, offset=offset),
        grid=(s // tile,),
        in_specs=[_const_spec((1, LANES))],
        out_specs=[pl.BlockSpec((tile, LANES), lambda i: (i, 0))] * 2,
        out_shape=[jax.ShapeDtypeStruct((s, LANES), F32)] * 2,
        compiler_params=pltpu.CompilerParams(dimension_semantics=("arbitrary",)),
        name="rope_tables",
    )(inv_lane)


def _rope(v, cos, sin_signed):
    partner = pltpu.roll(v, LANES - QK_ROPE_DIM // 2, 1)
    return v * cos + partner * sin_signed


def _expand_kv(ckv, kr_block, wk_ref, wvt_ref, k_ref, vt_ref):
    t = ckv.shape[0]
    k_in = jnp.concatenate([ckv.astype(BF16), kr_block.astype(BF16)], axis=1)
    k_all = jnp.dot(k_in, wk_ref[...], preferred_element_type=F32)
    for h in range(N_HEADS):
        k_ref[0, h] = k_all[:, h * HEAD_BLOCK:(h + 1) * HEAD_BLOCK].astype(BF16)
    vt = jnp.dot(wvt_ref[...], ckv.T.astype(BF16), preferred_element_type=F32)
    ones_row = (lax.broadcasted_iota(jnp.int32, (VT_ROWS, t), 0) == V_HEAD_DIM).astype(F32)
    for h in range(N_HEADS):
        vt_ref[0, h, 0] = (vt[h * VT_ROWS:(h + 1) * VT_ROWS, :] + ones_row).astype(BF16)


def _vt_spec(tile):
    return pl.BlockSpec((1, N_HEADS, 1, VT_ROWS, tile), lambda i, j: (i, 0, j, 0, 0))


def _mla_project_kernel(x_ref, g_ref, win_ref, qg_ref, wqb_ref, kvg_ref, wk_ref, wvt_ref,
                        cos_ref, sin_ref, q_ref, ckv_ref, kr_ref, *kv_refs):
    hb = _rms(x_ref[0], g_ref[...]).astype(BF16)
    o_kv = Q_LORA_RANK
    o_kr = o_kv + KV_LORA_RANK
    o_z = o_kr + HEAD_BLOCK
    p = jnp.dot(hb, win_ref[...], preferred_element_type=F32)

    ckv = _rms(p[:, o_kv:o_kr], kvg_ref[...])
    ckv_ref[0] = ckv

    cos = cos_ref[...]
    sin_signed = sin_ref[...]
    kr_block = _rope(p[:, o_kr:o_z], cos, sin_signed)
    kr_ref[0] = kr_block.T[ROPE_LO:ROPE_HI, :]

    qn = _rms(p[:, 0:o_kv], qg_ref[...]).astype(BF16)
    q2 = jnp.dot(qn, wqb_ref[...], preferred_element_type=F32)
    q, q_partner = q2[:, :N_HEADS * HEAD_BLOCK], q2[:, N_HEADS * HEAD_BLOCK:]
    cos_q = cos * Q_SCALE
    sin_q = sin_signed * Q_SCALE
    for h in range(N_HEADS):
        hs = slice(h * HEAD_BLOCK, (h + 1) * HEAD_BLOCK)
        q_ref[0, h] = (q[:, hs] * cos_q + q_partner[:, hs] * sin_q).astype(BF16)

    if kv_refs:
        _expand_kv(ckv, kr_block, wk_ref, wvt_ref, *kv_refs)


def _mla_project(x, g, w_in2, qg, wqb, kvg, wk, wvt, cos, sin_signed, tile, emit_kv):
    b, s, d = x.shape
    n = s // tile
    tok = lambda w: pl.BlockSpec((1, tile, w), lambda i, j: (i, j, 0))
    head = lambda nh: pl.BlockSpec((1, nh, tile, LANES), lambda i, j: (i, 0, j, 0))
    out_specs = [head(N_HEADS), tok(KV_LORA_RANK),
                 pl.BlockSpec((1, QK_ROPE_DIM, tile), lambda i, j: (i, 0, j))]
    out_shape = [
        jax.ShapeDtypeStruct((b, N_HEADS, s, HEAD_BLOCK), BF16),
        jax.ShapeDtypeStruct((b, s, KV_LORA_RANK), F32),
        jax.ShapeDtypeStruct((b, QK_ROPE_DIM, s), F32),
    ]
    if emit_kv:
        out_specs += [head(N_HEADS), _vt_spec(tile)]
        out_shape += [jax.ShapeDtypeStruct((b, N_HEADS, s, HEAD_BLOCK), BF16),
                      jax.ShapeDtypeStruct((b, N_HEADS, n, VT_ROWS, tile), BF16)]
    return pl.pallas_call(
        _mla_project_kernel,
        grid=(b, n),
        in_specs=[
            tok(d),
            _const_spec((1, d)),
            _const_spec(w_in2.shape),
            _const_spec(qg.shape),
            _const_spec(wqb.shape),
            _const_spec(kvg.shape),
            _const_spec(wk.shape),
            _const_spec(wvt.shape),
            pl.BlockSpec((tile, LANES), lambda i, j: (j, 0)),
            pl.BlockSpec((tile, LANES), lambda i, j: (j, 0)),
        ],
        out_specs=out_specs,
        out_shape=out_shape,
        compiler_params=pltpu.CompilerParams(
            dimension_semantics=("arbitrary", "arbitrary"),
            vmem_limit_bytes=VMEM_LIMIT),
        name="mla_project",
    )(x, g, w_in2, qg, wqb, kvg, wk, wvt, cos, sin_signed)


def _latent_attn_kernel(q_ref, ckv_ref, kr_ref, ckvn_ref, krn_ref, place_ref, wkt_ref, wvp_ref,
                        o_ref, qa_sc, m_sc, l_sc, acc_sc):
    j = pl.program_id(1)
    t = q_ref.shape[2]
    cols = N_HEADS * t
    col_tile = 2 * LANES

    @pl.when(j == 0)
    def _():
        lane = lax.broadcasted_iota(jnp.int32, (t, LANES), 1)
        is_rope = (lane >= ROPE_LO) & (lane < ROPE_HI)
        for h in range(N_HEADS):
            qh = q_ref[0, h]
            q_lat = jnp.dot(qh, wkt_ref[h], preferred_element_type=F32)
            qa_sc[h * t:(h + 1) * t, 0:LANES] = q_lat.astype(BF16)
            qa_sc[h * t:(h + 1) * t, LANES:2 * LANES] = jnp.where(is_rope, qh, jnp.zeros_like(qh))
        m_sc[...] = jnp.full(m_sc.shape, NEG_INF, F32)
        l_sc[...] = jnp.zeros(l_sc.shape, F32)
        acc_sc[...] = jnp.zeros(acc_sc.shape, F32)

    def attend(ckv, kr_t):
        tk = ckv.shape[0]
        kr_block = jnp.dot(place_ref[...], kr_t.astype(BF16), preferred_element_type=F32).T
        keys = jnp.concatenate([ckv.astype(BF16), kr_block.astype(BF16)], axis=1)
        ones_rows = (lax.broadcasted_iota(jnp.int32, (16, tk), 0) == 0).astype(F32)
        vals_t = jnp.concatenate([ckv.T, ones_rows], axis=0).astype(BF16)
        n_col = cols // col_tile
        col = lambda c: slice(c * col_tile, (c + 1) * col_tile)
        scores = lambda c: lax.dot_general(keys, qa_sc[col(c), :], (((1,), (1,)), ((), ())),
                                           preferred_element_type=F32)
        issued = {c: scores(c) for c in range(min(2, n_col))}
        for c in range(n_col):
            cs = col(c)
            st = issued.pop(c)
            if c + 2 < n_col:
                issued[c + 2] = scores(c + 2)
            m_prev = m_sc[:, cs]
            m_new = jnp.maximum(m_prev, jnp.max(st, axis=0, keepdims=True))
            alpha = jnp.exp2(m_prev - m_new)
            pt = jnp.exp2(st - m_new).astype(BF16)
            pv = jnp.dot(vals_t, pt, preferred_element_type=F32)
            l_sc[:, cs] = alpha * l_sc[:, cs] + pv[KV_LORA_RANK:KV_LORA_RANK + 1, :]
            m_sc[:, cs] = m_new
            acc_sc[:, cs] = acc_sc[:, cs] * alpha + pv[0:KV_LORA_RANK, :]

    attend(ckv_ref[0], kr_ref[0])

    @pl.when(j == pl.num_programs(1) - 1)
    def _():
        attend(ckvn_ref[0], krn_ref[0])
        o_lat = (acc_sc[...] / l_sc[...]).T.astype(BF16)
        for p in range(N_PAIRS):
            pair = sum(jnp.dot(o_lat[h * t:(h + 1) * t, :], wvp_ref[h],
                               preferred_element_type=F32) for h in (2 * p, 2 * p + 1))
            o_ref[0, p] = pair.astype(o_ref.dtype)


def _latent_attention(q, cache_ckv, cache_krt, ckv_new, krt_new, place, wkt, wvp, tile):
    b, nh, t, _ = q.shape
    past = cache_ckv.shape[1]
    cols = nh * t
    assert past % tile == 0 and cols % (2 * LANES) == 0
    return pl.pallas_call(
        _latent_attn_kernel,
        grid=(b, past // tile),
        in_specs=[
            pl.BlockSpec((1, nh, t, HEAD_BLOCK), lambda i, j: (i, 0, 0, 0)),
            pl.BlockSpec((1, tile, KV_LORA_RANK), lambda i, j: (i, j, 0)),
            pl.BlockSpec((1, QK_ROPE_DIM, tile), lambda i, j: (i, 0, j)),
            pl.BlockSpec((1, t, KV_LORA_RANK), lambda i, j: (i, 0, 0)),
            pl.BlockSpec((1, QK_ROPE_DIM, t), lambda i, j: (i, 0, 0)),
            _const_spec(place.shape), _const_spec(wkt.shape), _const_spec(wvp.shape),
        ],
        out_specs=pl.BlockSpec((1, N_PAIRS, t, LANES), lambda i, j: (i, 0, 0, 0)),
        out_shape=jax.ShapeDtypeStruct((b, N_PAIRS, t, LANES), BF16),
        scratch_shapes=[pltpu.VMEM((cols, 2 * LANES), BF16),
                        pltpu.VMEM((1, cols), F32),
                        pltpu.VMEM((1, cols), F32),
                        pltpu.VMEM((KV_LORA_RANK, cols), F32)],
        compiler_params=pltpu.CompilerParams(
            dimension_semantics=("arbitrary", "arbitrary"),
            vmem_limit_bytes=VMEM_LIMIT),
        name="latent_attention",
    )(q, cache_ckv, cache_krt, ckv_new, krt_new, place, wkt, wvp)


def _attn_kernel(q_ref, qn_ref, k_ref, vt_ref, o_ref, m_sc, l_sc, acc_sc, s_sc, mx_sc, d_sc):
    n_group, tq = q_ref.shape[1], q_ref.shape[2]
    tk = vt_ref.shape[4]
    i = pl.program_id(2)
    last_q = pl.num_programs(2) - 1

    m_sc[...] = jnp.full(m_sc.shape, NEG_INF, F32)
    l_sc[...] = jnp.zeros(l_sc.shape, F32)
    acc_sc[...] = jnp.zeros(acc_sc.shape, F32)

    def scores(qr, h, j):
        k = k_ref[0, h, pl.ds(pl.multiple_of(j * tk, tk), tk), :]
        return lax.dot_general(k, qr[0, h], (((1,), (1,)), ((), ())),
                               preferred_element_type=F32)

    def issue_scores(slot, h, j):
        st = scores(q_ref, h, j)
        s_sc[slot, h] = st
        mx_sc[slot, h] = jnp.max(st, axis=0, keepdims=True)

    def issue_next_diagonal(h):
        d_sc[h] = scores(qn_ref, h, jnp.minimum(i + 1, last_q))

    def pv_dot(vt, st, m):
        return jnp.dot(vt, jnp.exp2(st - m).astype(BF16), preferred_element_type=F32)

    def online_update(h, col_max, pv_of):
        pr, lo = h // 2, (h % 2) * V_HEAD_DIM
        m_prev = m_sc[h]
        m_new = jnp.maximum(m_prev, col_max)
        alpha = jnp.exp2(m_prev - m_new)
        pv = pv_of(m_new)
        l_sc[h] = alpha * l_sc[h] + pv[V_HEAD_DIM:V_HEAD_DIM + 1, :]
        m_sc[h] = m_new
        acc_sc[pr, lo:lo + V_HEAD_DIM, :] = (acc_sc[pr, lo:lo + V_HEAD_DIM, :] * alpha
                                             + pv[0:V_HEAD_DIM, :])

    def tile(j, slot, issue):
        for h in range(n_group):
            issue(h)
            online_update(h, mx_sc[slot, h],
                          lambda m, h=h: pv_dot(vt_ref[0, h, j], s_sc[slot, h], m))

    hk, hq = tk // 2, tq // 2
    quadrant_mask = (lax.broadcasted_iota(jnp.int32, (hk, hq), 0) // CHUNK
                     <= lax.broadcasted_iota(jnp.int32, (hk, hq), 1) // CHUNK)

    def diagonal_tile(h):
        mask = quadrant_mask
        st_aa = jnp.where(mask, d_sc[h, 0:hk, 0:hq], NEG_INF)
        st_ab = d_sc[h, 0:hk, hq:tq]
        st_bb = jnp.where(mask, d_sc[h, hk:tk, hq:tq], NEG_INF)
        col_max = jnp.concatenate(
            [jnp.max(st_aa, axis=0, keepdims=True),
             jnp.maximum(jnp.max(st_ab, axis=0, keepdims=True),
                         jnp.max(st_bb, axis=0, keepdims=True))], axis=1)
        vt_a = vt_ref[0, h, i, :, 0:hk]
        vt_b = vt_ref[0, h, i, :, hk:tk]
        online_update(h, col_max, lambda m: jnp.concatenate(
            [pv_dot(vt_a, st_aa, m[:, 0:hq]),
             pv_dot(vt_a, st_ab, m[:, hq:tq]) + pv_dot(vt_b, st_bb, m[:, hq:tq])], axis=1))

    @pl.when(i == 0)
    def _():
        for h in range(n_group):
            d_sc[h] = scores(q_ref, h, 0)

    for h in range(n_group):
        issue_scores(0, h, 0)
        diagonal_tile(h)

    def tiles_from(j0, count):
        for d in range(count):
            tile(j0 + d, d % 2, lambda h, d=d: issue_scores((d + 1) % 2, h, j0 + d + 1))

    def body4(jj, carry):
        tiles_from(4 * jj, 4)
        return carry

    def body2(jj, carry):
        tiles_from(2 * jj, 2)
        return carry

    def body1(j, carry):
        tiles_from(j, 1)
        return carry

    n_loop = jnp.maximum(i - 1, 0)
    n_quads, n_pairs = n_loop // 4, n_loop // 2
    lax.fori_loop(0, n_quads, body4, 0)
    lax.fori_loop(2 * n_quads, n_pairs, body2, 0)
    lax.fori_loop(2 * n_pairs, n_loop, body1, 0)

    def write_output():
        for pr in range(n_group // 2):
            inv_l = jnp.concatenate(
                [jnp.broadcast_to(1.0 / l_sc[2 * pr + hh], (V_HEAD_DIM, tq)) for hh in range(2)],
                axis=0)
            o_ref[0, pr] = (acc_sc[pr] * inv_l).T.astype(o_ref.dtype)

    @pl.when(i > 0)
    def _():
        tile(i - 1, (i - 1) % 2, issue_next_diagonal)
        write_output()

    @pl.when(i == 0)
    def _():
        for h in range(n_group):
            issue_next_diagonal(h)
        write_output()


def _attention(q, k, vt, group):
    b, nh, sq, _ = q.shape
    skv = k.shape[2]
    n_kv, tk = vt.shape[2], vt.shape[4]
    tq = tk
    assert n_kv * tk == skv == sq and (tk // 2) % CHUNK == 0
    gp = group // 2
    n_q = sq // tq
    return pl.pallas_call(
        _attn_kernel,
        grid=(b, nh // group, n_q),
        in_specs=[
            pl.BlockSpec((1, group, tq, HEAD_BLOCK), lambda i, g, j: (i, g, j, 0)),
            pl.BlockSpec((1, group, tq, HEAD_BLOCK),
                         lambda i, g, j: (i, g, jnp.minimum(j + 1, n_q - 1), 0)),
            pl.BlockSpec((1, group, skv, HEAD_BLOCK), lambda i, g, j: (i, g, 0, 0)),
            pl.BlockSpec((1, group, n_kv, VT_ROWS, tk), lambda i, g, j: (i, g, 0, 0, 0)),
        ],
        out_specs=pl.BlockSpec((1, gp, tq, LANES), lambda i, g, j: (i, g, j, 0)),
        out_shape=jax.ShapeDtypeStruct((b, nh // 2, sq, LANES), BF16),
        scratch_shapes=[pltpu.VMEM((group, 1, tq), F32),
                        pltpu.VMEM((group, 1, tq), F32),
                        pltpu.VMEM((gp, LANES, tq), F32),
                        pltpu.VMEM((2, group, tk, tq), F32),
                        pltpu.VMEM((2, group, 1, tq), F32),
                        pltpu.VMEM((group, tk, tq), F32)],
        compiler_params=pltpu.CompilerParams(
            dimension_semantics=("arbitrary", "arbitrary", "arbitrary"),
            vmem_limit_bytes=VMEM_LIMIT),
        name="attention",
    )(q, q, k, vt)


def _out_kernel(o_ref, x_ref, g1_ref, wz_ref, w_ref, g_ref, y_ref):
    t = x_ref.shape[1]
    parts = 2 if t >= 4 * LANES else 1
    rows = [slice(p * t // parts, (p + 1) * t // parts) for p in range(parts)]
    xs = [x_ref[0, r, :] for r in rows]
    zs = [jnp.dot(_rms(x, g1_ref[...]).astype(BF16), wz_ref[...], preferred_element_type=F32)
          for x in xs]
    for r, x, z in zip(rows, xs, zs):
        o = jnp.concatenate([o_ref[0, p, r, :] for p in range(N_PAIRS)], axis=1)
        gated = (z * jax.nn.sigmoid(z) * o.astype(F32)).astype(BF16)
        x2 = x + jnp.dot(gated, w_ref[...], preferred_element_type=F32)
        y_ref[0, r, :] = _rms(x2, g_ref[...])


def _out_layer(o, x, g1, w_z, w_out, g, tile):
    b, s, d = x.shape
    n = s // tile
    tok = lambda w: pl.BlockSpec((1, tile, w), lambda i, j: (i, j, 0))
    return pl.pallas_call(
        _out_kernel,
        grid=(b, n),
        in_specs=[pl.BlockSpec((1, N_PAIRS, tile, LANES), lambda i, j: (i, 0, j, 0)),
                  tok(d), _const_spec((1, d)), _const_spec(w_z.shape),
                  _const_spec(w_out.shape), _const_spec((1, d))],
        out_specs=tok(d),
        out_shape=jax.ShapeDtypeStruct((b, s, d), F32),
        compiler_params=pltpu.CompilerParams(
            dimension_semantics=("arbitrary", "arbitrary"),
            vmem_limit_bytes=VMEM_LIMIT),
        name="out_layer",
    )(o, x, g1, w_z, w_out, g)


def _rope_block_cols(w):
    half = QK_ROPE_DIM // 2
    fill = jnp.zeros(w.shape[:-1] + (HEAD_BLOCK - ROPE_HI - half,), w.dtype)
    return jnp.concatenate([w, w[..., :half], fill], axis=-1)


def _prep_weights(mla_w_in, mla_w_qb, mla_w_kvb):
    o_kv = Q_LORA_RANK
    o_kr = o_kv + KV_LORA_RANK
    o_z = o_kr + QK_ROPE_DIM
    d = mla_w_in.shape[0]
    kr_cols = jnp.concatenate(
        [jnp.zeros((d, ROPE_LO), F32), _rope_block_cols(mla_w_in[:, o_kr:o_z])], axis=1)
    w_in2 = jnp.concatenate([mla_w_in[:, :o_kr], kr_cols], axis=1)
    w_z = mla_w_in[:, o_z:]

    wqb = mla_w_qb.reshape(Q_LORA_RANK, N_HEADS, QK_DIM)
    half = QK_ROPE_DIM // 2
    x1, x2 = wqb[..., QK_NOPE_DIM:QK_NOPE_DIM + half], wqb[..., QK_NOPE_DIM + half:]
    lane_pad = ((0, 0), (0, 0), (0, HEAD_BLOCK - QK_DIM))
    wqb_main = jnp.pad(wqb, lane_pad)
    wqb_partner = jnp.pad(jnp.concatenate([jnp.zeros_like(wqb[..., :QK_NOPE_DIM]), x2, x1], axis=-1),
                          lane_pad)
    wqb = jnp.concatenate([wqb_main.reshape(Q_LORA_RANK, N_HEADS * HEAD_BLOCK),
                           wqb_partner.reshape(Q_LORA_RANK, N_HEADS * HEAD_BLOCK)], axis=1)

    wkvb = mla_w_kvb.reshape(KV_LORA_RANK, N_HEADS, QK_NOPE_DIM + V_HEAD_DIM)
    wk_pad = jnp.pad(wkvb[:, :, :QK_NOPE_DIM], ((0, 0), (0, 0), (0, HEAD_BLOCK - QK_NOPE_DIM)))
    place = np.zeros((HEAD_BLOCK, N_HEADS, HEAD_BLOCK), np.float32)
    lanes = np.arange(ROPE_LO, ROPE_HI)
    place[lanes, :, lanes] = 1.0
    wk = jnp.concatenate([wk_pad, place], axis=0).reshape(KV_LORA_RANK + HEAD_BLOCK,
                                                          N_HEADS * HEAD_BLOCK)
    wv = wkvb[:, :, QK_NOPE_DIM:]
    wvt = jnp.pad(jnp.transpose(wv, (1, 2, 0)), ((0, 0), (0, VT_ROWS - V_HEAD_DIM), (0, 0)))
    wvt = wvt.reshape(N_HEADS * VT_ROWS, KV_LORA_RANK)

    wkt = jnp.transpose(wk_pad, (1, 2, 0))
    wv_lo = jnp.pad(wv, ((0, 0), (0, 0), (0, LANES - V_HEAD_DIM)))
    wv_hi = jnp.pad(wv, ((0, 0), (0, 0), (LANES - V_HEAD_DIM, 0)))
    odd = (jnp.arange(N_HEADS) % 2 == 1)[None, :, None]
    wvp = jnp.transpose(jnp.where(odd, wv_hi, wv_lo), (1, 0, 2))
    return tuple(w.astype(BF16) for w in (w_in2, w_z, wqb, wk, wvt, wkt, wvp))


def _inv_freq_lanes():
    inv = 1.0 / (ROPE_THETA ** (jnp.arange(0, QK_ROPE_DIM, 2, dtype=F32) / QK_ROPE_DIM))
    z = lambda n: jnp.zeros((n,), F32)
    return jnp.concatenate([z(ROPE_LO), inv, inv, z(HEAD_BLOCK - ROPE_HI)])[None, :]


def _rope_placement():
    place = np.zeros((HEAD_BLOCK, QK_ROPE_DIM), np.float32)
    rows = np.arange(QK_ROPE_DIM)
    place[ROPE_LO + rows, rows] = 1.0
    return jnp.asarray(place, BF16)


def kernel(x_prompt, x_sample, state_conv, cache_ckv, cache_krope, norm_g, final_norm_g,
           conv_w_in, conv_w, conv_w_out, mla_w_in, mla_q_norm_g, mla_w_qb,
           mla_kv_norm_g, mla_w_kvb, mla_w_out):
    bp, s, d = x_prompt.shape
    bs, t, _ = x_sample.shape
    past = cache_ckv.shape[2]

    tok_tile = 512
    attn_tile = 512
    cache_tile = 1024
    group = 4

    g0 = norm_g[0][None, :]
    g1 = norm_g[1][None, :]
    gf = final_norm_g[None, :]
    cw_in = conv_w_in[0].astype(BF16)
    cw_out = conv_w_out[0].astype(BF16)
    w_in2, w_z, wqb, wk, wvt, wkt, wvp = _prep_weights(mla_w_in[0], mla_w_qb[0], mla_w_kvb[0])
    qg = mla_q_norm_g[0][None, :]
    kvg = mla_kv_norm_g[0][None, :]
    w_out = mla_w_out[0].astype(BF16)
    inv_lane = _inv_freq_lanes()

    zeros_prev = jnp.zeros((bp, CONV_WIDTH - 1, d), F32)
    xp1, st_p = _conv_layer(x_prompt, zeros_prev, g0, cw_in, conv_w[0], cw_out, tok_tile)
    xs1, st_s = _conv_layer(x_sample, state_conv[0], g0, cw_in, conv_w[0], cw_out, t)

    cos_p, sin_p = _rope_tables(inv_lane, s, 0, 512)
    cos_s, sin_s = _rope_tables(inv_lane, t, past, t)

    qp, ckv_p, krt_p, kp, vp = _mla_project(
        xp1, g1, w_in2, qg, wqb, kvg, wk, wvt, cos_p, sin_p, attn_tile, True)
    qs, ckv_s, krt_s = _mla_project(
        xs1, g1, w_in2, qg, wqb, kvg, wk, wvt, cos_s, sin_s, t, False)

    op = _attention(qp, kp, vp, group)
    cache_krt = jnp.swapaxes(cache_krope[0], 1, 2)
    os_ = _latent_attention(qs, cache_ckv[0], cache_krt, ckv_s, krt_s,
                            _rope_placement(), wkt, wvp, cache_tile)

    y_prompt = _out_layer(op, xp1, g1, w_z, w_out, gf, tok_tile)
    y_sample = _out_layer(os_, xs1, g1, w_z, w_out, gf, t)

    return (y_prompt, y_sample, st_p[None], ckv_p[None], jnp.swapaxes(krt_p, 1, 2)[None],
            st_s[None], ckv_s[None], jnp.swapaxes(krt_s, 1, 2)[None])
```

```python
import functools
import math

import jax
import jax.numpy as jnp
import numpy as np
from jax import lax
from jax.experimental import pallas as pl
from jax.experimental.pallas import tpu as pltpu

D_MODEL = 1024
CHUNK = 64
CONV_WIDTH = 3
N_HEADS = 16
QK_NOPE_DIM = 64
QK_ROPE_DIM = 32
V_HEAD_DIM = 64
QK_DIM = QK_NOPE_DIM + QK_ROPE_DIM
Q_LORA_RANK = 256
KV_LORA_RANK = 128
MLA_GATE_DIM = N_HEADS * V_HEAD_DIM
ROPE_THETA = 10000.0
EPS = 1e-6
NEG_INF = -1e30
PAST_LEN = 4096

LANES = 128
HEAD_BLOCK = LANES
N_PAIRS = N_HEADS // 2
VT_ROWS = V_HEAD_DIM + 16
ROPE_LO = QK_NOPE_DIM
ROPE_MID = ROPE_LO + QK_ROPE_DIM // 2
ROPE_HI = ROPE_LO + QK_ROPE_DIM
Q_SCALE = (QK_DIM ** -0.5) * math.log2(math.e)

VMEM_LIMIT =56 * 1024 * 1024
F32 = jnp.float32
BF16 = jnp.bfloat16


def _rms(x, g):
    ms = jnp.mean(x * x, axis=-1, keepdims=True)
    return x * lax.rsqrt(ms + EPS) * g


def _const_spec(shape):
    nd = len(shape)
    return pl.BlockSpec(shape, lambda *_: (0,) * nd)


def _conv_kernel(x_ref, prev_ref, g_ref, win_ref, cw_ref, wout_ref, xo_ref, st_ref, ubuf):
    j = pl.program_id(1)
    t = x_ref.shape[1]
    halo = CONV_WIDTH - 1
    base = 8

    @pl.when(j == 0)
    def _():
        ubuf[base - halo:base, :] = prev_ref[0]

    x = x_ref[0]
    hb = _rms(x, g_ref[...]).astype(BF16)

    def proj(k):
        return jnp.dot(hb, win_ref[:, k * D_MODEL:(k + 1) * D_MODEL],
                       preferred_element_type=F32)

    u = proj(1) * proj(2)
    ubuf[base:base + t, :] = u
    conv = (ubuf[base - 2:base - 2 + t, :] * cw_ref[0:1, :]
            + ubuf[base - 1:base - 1 + t, :] * cw_ref[1:2, :]
            + u * cw_ref[2:3, :])
    new_halo = ubuf[base + t - halo:base + t, :]
    ubuf[base - halo:base, :] = new_halo
    z = proj(3)
    gated = (z * jax.nn.sigmoid(z)) * proj(0) * conv
    y = jnp.dot(gated.astype(BF16), wout_ref[...], preferred_element_type=F32)
    xo_ref[0] = x + y

    @pl.when(j == pl.num_programs(1) - 1)
    def _():
        st_ref[0] = new_halo


def _conv_layer(x, prev, g, w_in, conv_w, w_out, tile):
    b, s, d = x.shape
    n = s // tile
    return pl.pallas_call(
        _conv_kernel,
        grid=(b, n),
        in_specs=[
            pl.BlockSpec((1, tile, d), lambda i, j: (i, j, 0)),
            pl.BlockSpec((1, CONV_WIDTH - 1, d), lambda i, j: (i, 0, 0)),
            _const_spec((1, d)),
            _const_spec(w_in.shape),
            _const_spec(conv_w.shape),
            _const_spec(w_out.shape),
        ],
        out_specs=[
            pl.BlockSpec((1, tile, d), lambda i, j: (i, j, 0)),
            pl.BlockSpec((1, CONV_WIDTH - 1, d), lambda i, j: (i, 0, 0)),
        ],
        out_shape=[
            jax.ShapeDtypeStruct((b, s, d), F32),
            jax.ShapeDtypeStruct((b, CONV_WIDTH - 1, d), F32),
        ],
        scratch_shapes=[pltpu.VMEM((tile + 8, d), F32)],
        compiler_params=pltpu.CompilerParams(
            dimension_semantics=("arbitrary", "arbitrary"),
            vmem_limit_bytes=VMEM_LIMIT),
        name="conv_layer",
    )(x, prev, g, w_in, conv_w, w_out)


def _rope_table_kernel(inv_ref, cos_ref, sin_ref, *, offset):
    t = cos_ref.shape[0]
    row = lax.broadcasted_iota(jnp.int32, (t, LANES), 0)
    lane = lax.broadcasted_iota(jnp.int32, (t, LANES), 1)
    pos = (offset + pl.program_id(0) * t + row).astype(F32)
    ang = pos * inv_ref[...]
    cos_ref[...] = jnp.cos(ang)
    s = jnp.sin(ang)
    sin_ref[...] = jnp.where(lane < ROPE_MID, -s, s)


def _rope_tables(inv_lane, s, offset, tile):
    return pl.pallas_call(
        functools.partial(_rope_table_kernel, offset=offset),
        grid=(s // tile,),
        in_specs=[_const_spec((1, LANES))],
        out_specs=[pl.BlockSpec((tile, LANES), lambda i: (i, 0))] * 2,
        out_shape=[jax.ShapeDtypeStruct((s, LANES), F32)] * 2,
        compiler_params=pltpu.CompilerParams(dimension_semantics=("arbitrary",)),
        name="rope_tables",
    )(inv_lane)


def _rope(v, cos, sin_signed):
    partner = pltpu.roll(v, LANES - QK_ROPE_DIM // 2, 1)
    return v * cos + partner * sin_signed


def _expand_kv(ckv, kr_block, wk_ref, wvt_ref, k_ref, vt_ref):
    t = ckv.shape[0]
    k_in = jnp.concatenate([ckv.astype(BF16), kr_block.astype(BF16)], axis=1)
    k_all = jnp.dot(k_in, wk_ref[...], preferred_element_type=F32)
    for h in range(N_HEADS):
        k_ref[0, h] = k_all[:, h * HEAD_BLOCK:(h + 1) * HEAD_BLOCK].astype(BF16)
    vt = jnp.dot(wvt_ref[...], ckv.T.astype(BF16), preferred_element_type=F32)
    ones_row = (lax.broadcasted_iota(jnp.int32, (VT_ROWS, t), 0) == V_HEAD_DIM).astype(F32)
    for h in range(N_HEADS):
        vt_ref[0, h, 0] = (vt[h * VT_ROWS:(h + 1) * VT_ROWS, :] + ones_row).astype(BF16)


def _vt_spec(tile):
    return pl.BlockSpec((1, N_HEADS, 1, VT_ROWS, tile), lambda i, j: (i, 0, j, 0, 0))


def _mla_project_kernel(x_ref, g_ref, win_ref, qg_ref, wqb_ref, kvg_ref, wk_ref, wvt_ref,
                        cos_ref, sin_ref, q_ref, ckv_ref, kr_ref, *kv_refs):
    hb = _rms(x_ref[0], g_ref[...]).astype(BF16)
    o_kv = Q_LORA_RANK
    o_kr = o_kv + KV_LORA_RANK
    o_z = o_kr + HEAD_BLOCK
    p = jnp.dot(hb, win_ref[...], preferred_element_type=F32)

    ckv = _rms(p[:, o_kv:o_kr], kvg_ref[...])
    ckv_ref[0] = ckv

    cos = cos_ref[...]
    sin_signed = sin_ref[...]
    kr_block = _rope(p[:, o_kr:o_z], cos, sin_signed)
    kr_ref[0] = kr_block.T[ROPE_LO:ROPE_HI, :]

    qn = _rms(p[:, 0:o_kv], qg_ref[...]).astype(BF16)
    q2 = jnp.dot(qn, wqb_ref[...], preferred_element_type=F32)
    q, q_partner = q2[:, :N_HEADS * HEAD_BLOCK], q2[:, N_HEADS * HEAD_BLOCK:]
    cos_q = cos * Q_SCALE
    sin_q = sin_signed * Q_SCALE
    for h in range(N_HEADS):
        hs = slice(h * HEAD_BLOCK, (h + 1) * HEAD_BLOCK)
        q_ref[0, h] = (q[:, hs] * cos_q + q_partner[:, hs] * sin_q).astype(BF16)

    if kv_refs:
        _expand_kv(ckv, kr_block, wk_ref, wvt_ref, *kv_refs)


def _mla_project(x, g, w_in2, qg, wqb, kvg, wk, wvt, cos, sin_signed, tile, emit_kv):
    b, s, d = x.shape
    n = s // tile
    tok = lambda w: pl.BlockSpec((1, tile, w), lambda i, j: (i, j, 0))
    head = lambda nh: pl.BlockSpec((1, nh, tile, LANES), lambda i, j: (i, 0, j, 0))
    out_specs = [head(N_HEADS), tok(KV_LORA_RANK),
                 pl.BlockSpec((1, QK_ROPE_DIM, tile), lambda i, j: (i, 0, j))]
    out_shape = [
        jax.ShapeDtypeStruct((b, N_HEADS, s, HEAD_BLOCK), BF16),
        jax.ShapeDtypeStruct((b, s, KV_LORA_RANK), F32),
        jax.ShapeDtypeStruct((b, QK_ROPE_DIM, s), F32),
    ]
    if emit_kv:
        out_specs += [head(N_HEADS), _vt_spec(tile)]
        out_shape += [jax.ShapeDtypeStruct((b, N_HEADS, s, HEAD_BLOCK), BF16),
                      jax.ShapeDtypeStruct((b, N_HEADS, n, VT_ROWS, tile), BF16)]
    return pl.pallas_call(
        _mla_project_kernel,
        grid=(b, n),
        in_specs=[
            tok(d),
            _const_spec((1, d)),
            _const_spec(w_in2.shape),
            _const_spec(qg.shape),
            _const_spec(wqb.shape),
            _const_spec(kvg.shape),
            _const_spec(wk.shape),
            _const_spec(wvt.shape),
            pl.BlockSpec((tile, LANES), lambda i, j: (j, 0)),
            pl.BlockSpec((tile, LANES), lambda i, j: (j, 0)),
        ],
        out_specs=out_specs,
        out_shape=out_shape,
        compiler_params=pltpu.CompilerParams(
            dimension_semantics=("arbitrary", "arbitrary"),
            vmem_limit_bytes=VMEM_LIMIT),
        name="mla_project",
    )(x, g, w_in2, qg, wqb, kvg, wk, wvt, cos, sin_signed)


def _latent_attn_kernel(q_ref, ckv_ref, kr_ref, ckvn_ref, krn_ref, place_ref, wkt_ref, wvp_ref,
                        o_ref, qa_sc, m_sc, l_sc, acc_sc, st_sc):
    j = pl.program_id(1)
    t = q_ref.shape[2]
    cols = N_HEADS * t
    col_tile = 2 * LANES

    @pl.when(j == 0)
    def _():
        lane = lax.broadcasted_iota(jnp.int32, (t, LANES), 1)
        is_rope = (lane >= ROPE_LO) & (lane < ROPE_HI)
        for h in range(N_HEADS):
            qh = q_ref[0, h]
            q_lat = jnp.dot(qh, wkt_ref[h], preferred_element_type=F32)
            qa_sc[h * t:(h + 1) * t, 0:LANES] = q_lat.astype(BF16)
            qa_sc[h * t:(h + 1) * t, LANES:2 * LANES] = jnp.where(is_rope, qh, jnp.zeros_like(qh))
        m_sc[...] = jnp.full(m_sc.shape, NEG_INF, F32)
        l_sc[...] = jnp.zeros(l_sc.shape, F32)
        acc_sc[...] = jnp.zeros(acc_sc.shape, F32)

    def attend(ckv, kr_t):
        tk = ckv.shape[0]
        kr_block = jnp.dot(place_ref[...], kr_t.astype(BF16), preferred_element_type=F32).T
        keys = jnp.concatenate([ckv.astype(BF16), kr_block.astype(BF16)], axis=1)
        ones_rows = (lax.broadcasted_iota(jnp.int32, (16, tk), 0) == 0).astype(F32)
        vals_t = jnp.concatenate([ckv.T, ones_rows], axis=0).astype(BF16)
        n_col = cols // col_tile
        col = lambda c: slice(c * col_tile, (c + 1) * col_tile)
        scores = lambda c: lax.dot_general(keys, qa_sc[col(c), :], (((1,), (1,)), ((), ())),
                                           preferred_element_type=F32)
        col_max = []
        for c in range(n_col):
            st = scores(c)
            st_sc[c, 0:tk, :] = st
            col_max.append(jnp.max(st, axis=0, keepdims=True))
        for c in range(n_col):
            cs = col(c)
            m_prev = m_sc[:, cs]
            m_new = jnp.maximum(m_prev, col_max[c])
            alpha = jnp.exp2(m_prev - m_new)
            pt = jnp.exp2(st_sc[c, 0:tk, :] - m_new).astype(BF16)
            pv = jnp.dot(vals_t, pt, preferred_element_type=F32)
            l_sc[:, cs] = alpha * l_sc[:, cs] + pv[KV_LORA_RANK:KV_LORA_RANK + 1, :]
            m_sc[:, cs] = m_new
            acc_sc[:, cs] = acc_sc[:, cs] * alpha + pv[0:KV_LORA_RANK, :]

    attend(ckv_ref[0], kr_ref[0])

    @pl.when(j == pl.num_programs(1) - 1)
    def _():
        attend(ckvn_ref[0], krn_ref[0])
        o_lat = (acc_sc[...] / l_sc[...]).T.astype(BF16)
        for p in range(N_PAIRS):
            pair = sum(jnp.dot(o_lat[h * t:(h + 1) * t, :], wvp_ref[h],
                               preferred_element_type=F32) for h in (2 * p, 2 * p + 1))
            o_ref[0, p] = pair.astype(o_ref.dtype)


def _latent_attention(q, cache_ckv, cache_krt, ckv_new, krt_new, place, wkt, wvp, tile):
    b, past = cache_ckv.shape[0], cache_ckv.shape[1]
    nh, t = q.shape[1], q.shape[2] // b
    cols = nh * t
    assert past % tile == 0 and cols % (2 * LANES) == 0
    return pl.pallas_call(
        _latent_attn_kernel,
        grid=(b, past // tile),
        in_specs=[
            pl.BlockSpec((1, nh, t, HEAD_BLOCK), lambda i, j: (0, 0, i, 0)),
            pl.BlockSpec((1, tile, KV_LORA_RANK), lambda i, j: (i, j, 0)),
            pl.BlockSpec((1, QK_ROPE_DIM, tile), lambda i, j: (i, 0, j)),
            pl.BlockSpec((1, t, KV_LORA_RANK), lambda i, j: (0, i, 0)),
            pl.BlockSpec((1, QK_ROPE_DIM, t), lambda i, j: (i, 0, 0)),
            _const_spec(place.shape), _const_spec(wkt.shape), _const_spec(wvp.shape),
        ],
        out_specs=pl.BlockSpec((1, N_PAIRS, t, LANES), lambda i, j: (0, 0, i, 0)),
        out_shape=jax.ShapeDtypeStruct((1, N_PAIRS, b * t, LANES), BF16),
        scratch_shapes=[pltpu.VMEM((cols, 2 * LANES), BF16),
                        pltpu.VMEM((1, cols), F32),
                        pltpu.VMEM((1, cols), F32),
                        pltpu.VMEM((KV_LORA_RANK, cols), F32),
                        pltpu.VMEM((cols // (2 * LANES), tile, 2 * LANES), F32)],
        compiler_params=pltpu.CompilerParams(
            dimension_semantics=("arbitrary", "arbitrary"),
            vmem_limit_bytes=VMEM_LIMIT),
        name="latent_attention",
    )(q, cache_ckv, cache_krt, ckv_new, krt_new, place, wkt, wvp)


def _attn_kernel(q_ref, qn_ref, k_ref, vt_ref, o_ref, m_sc, l_sc, acc_sc, s_sc, mx_sc, d_sc):
    n_group, tq = q_ref.shape[1], q_ref.shape[2]
    tk = vt_ref.shape[4]
    i = pl.program_id(2)
    last_q = pl.num_programs(2) - 1

    m_sc[...] = jnp.full(m_sc.shape, NEG_INF, F32)
    l_sc[...] = jnp.zeros(l_sc.shape, F32)
    acc_sc[...] = jnp.zeros(acc_sc.shape, F32)

    def scores(qr, h, j):
        k = k_ref[0, h, pl.ds(pl.multiple_of(j * tk, tk), tk), :]
        return lax.dot_general(k, qr[0, h], (((1,), (1,)), ((), ())),
                               preferred_element_type=F32)

    def issue_scores(slot, h, j):
        st = scores(q_ref, h, j)
        s_sc[slot, h] = st
        mx_sc[slot, h] = jnp.max(st, axis=0, keepdims=True)

    def issue_next_diagonal(h):
        d_sc[h] = scores(qn_ref, h, jnp.minimum(i + 1, last_q))

    def pv_dot(vt, st, m):
        return jnp.dot(vt, jnp.exp2(st - m).astype(BF16), preferred_element_type=F32)

    def online_update(h, col_max, pv_of):
        pr, lo = h // 2, (h % 2) * V_HEAD_DIM
        m_prev = m_sc[h]
        m_new = jnp.maximum(m_prev, col_max)
        alpha = jnp.exp2(m_prev - m_new)
        pv = pv_of(m_new)
        l_sc[h] = alpha * l_sc[h] + pv[V_HEAD_DIM:V_HEAD_DIM + 1, :]
        m_sc[h] = m_new
        acc_sc[pr, lo:lo + V_HEAD_DIM, :] = (acc_sc[pr, lo:lo + V_HEAD_DIM, :] * alpha
                                             + pv[0:V_HEAD_DIM, :])

    def tile(j, slot, issue):
        for h in range(n_group):
            issue(h)
            online_update(h, mx_sc[slot, h],
                          lambda m, h=h: pv_dot(vt_ref[0, h, j], s_sc[slot, h], m))

    hk, hq = tk // 2, tq // 2
    quadrant_mask = (lax.broadcasted_iota(jnp.int32, (hk, hq), 0) // CHUNK
                     <= lax.broadcasted_iota(jnp.int32, (hk, hq), 1) // CHUNK)

    def diagonal_tile(h):
        mask = quadrant_mask
        st_aa = jnp.where(mask, d_sc[h, 0:hk, 0:hq], NEG_INF)
        st_ab = d_sc[h, 0:hk, hq:tq]
        st_bb = jnp.where(mask, d_sc[h, hk:tk, hq:tq], NEG_INF)
        col_max = jnp.concatenate(
            [jnp.max(st_aa, axis=0, keepdims=True),
             jnp.maximum(jnp.max(st_ab, axis=0, keepdims=True),
                         jnp.max(st_bb, axis=0, keepdims=True))], axis=1)
        vt_a = vt_ref[0, h, i, :, 0:hk]
        vt_b = vt_ref[0, h, i, :, hk:tk]
        online_update(h, col_max, lambda m: jnp.concatenate(
            [pv_dot(vt_a, st_aa, m[:, 0:hq]),
             pv_dot(vt_a, st_ab, m[:, hq:tq]) + pv_dot(vt_b, st_bb, m[:, hq:tq])], axis=1))

    @pl.when(i == 0)
    def _():
        for h in range(n_group):
            d_sc[h] = scores(q_ref, h, 0)

    for h in range(n_group):
        issue_scores(0, h, 0)
        diagonal_tile(h)

    def tiles_from(j0, count):
        for d in range(count):
            tile(j0 + d, d % 2, lambda h, d=d: issue_scores((d + 1) % 2, h, j0 + d + 1))

    def body4(jj, carry):
        tiles_from(4 * jj, 4)
        return carry

    def body2(jj, carry):
        tiles_from(2 * jj, 2)
        return carry

    def body1(j, carry):
        tiles_from(j, 1)
        return carry

    n_loop = jnp.maximum(i - 1, 0)
    n_quads, n_pairs = n_loop // 4, n_loop // 2
    lax.fori_loop(0, n_quads, body4, 0)
    lax.fori_loop(2 * n_quads, n_pairs, body2, 0)
    lax.fori_loop(2 * n_pairs, n_loop, body1, 0)

    def write_output():
        for pr in range(n_group // 2):
            inv_l = jnp.concatenate(
                [jnp.broadcast_to(1.0 / l_sc[2 * pr + hh], (V_HEAD_DIM, tq)) for hh in range(2)],
                axis=0)
            o_ref[0, pr] = (acc_sc[pr] * inv_l).T.astype(o_ref.dtype)

    @pl.when(i > 0)
    def _():
        tile(i - 1, (i - 1) % 2, issue_next_diagonal)
        write_output()

    @pl.when(i == 0)
    def _():
        for h in range(n_group):
            issue_next_diagonal(h)
        write_output()


def _attention(q, k, vt, group):
    b, nh, sq, _ = q.shape
    skv = k.shape[2]
    n_kv, tk = vt.shape[2], vt.shape[4]
    tq = tk
    assert n_kv * tk == skv == sq and (tk // 2) % CHUNK == 0
    gp = group // 2
    n_q = sq // tq
    return pl.pallas_call(
        _attn_kernel,
        grid=(b, nh // group, n_q),
        in_specs=[
            pl.BlockSpec((1, group, tq, HEAD_BLOCK), lambda i, g, j: (i, g, j, 0)),
            pl.BlockSpec((1, group, tq, HEAD_BLOCK),
                         lambda i, g, j: (i, g, jnp.minimum(j + 1, n_q - 1), 0)),
            pl.BlockSpec((1, group, skv, HEAD_BLOCK), lambda i, g, j: (i, g, 0, 0)),
            pl.BlockSpec((1, group, n_kv, VT_ROWS, tk), lambda i, g, j: (i, g, 0, 0, 0)),
        ],
        out_specs=pl.BlockSpec((1, gp, tq, LANES), lambda i, g, j: (i, g, j, 0)),
        out_shape=jax.ShapeDtypeStruct((b, nh // 2, sq, LANES), BF16),
        scratch_shapes=[pltpu.VMEM((group, 1, tq), F32),
                        pltpu.VMEM((group, 1, tq), F32),
                        pltpu.VMEM((gp, LANES, tq), F32),
                        pltpu.VMEM((2, group, tk, tq), F32),
                        pltpu.VMEM((2, group, 1, tq), F32),
                        pltpu.VMEM((group, tk, tq), F32)],
        compiler_params=pltpu.CompilerParams(
            dimension_semantics=("arbitrary", "arbitrary", "arbitrary"),
            vmem_limit_bytes=VMEM_LIMIT),
        name="attention",
    )(q, q, k, vt)


def _out_kernel(o_ref, x_ref, g1_ref, wz_ref, w_ref, g_ref, y_ref):
    t = x_ref.shape[1]
    parts = 2 if t >= 4 * LANES else 1
    rows = [slice(p * t // parts, (p + 1) * t // parts) for p in range(parts)]
    xs = [x_ref[0, r, :] for r in rows]
    zs = [jnp.dot(_rms(x, g1_ref[...]).astype(BF16), wz_ref[...], preferred_element_type=F32)
          for x in xs]
    for r, x, z in zip(rows, xs, zs):
        o = jnp.concatenate([o_ref[0, p, r, :] for p in range(N_PAIRS)], axis=1)
        gated = (z * jax.nn.sigmoid(z) * o.astype(F32)).astype(BF16)
        x2 = x + jnp.dot(gated, w_ref[...], preferred_element_type=F32)
        y_ref[0, r, :] = _rms(x2, g_ref[...])


def _out_layer(o, x, g1, w_z, w_out, g, tile):
    b, s, d = x.shape
    n = s // tile
    tok = lambda w: pl.BlockSpec((1, tile, w), lambda i, j: (i, j, 0))
    return pl.pallas_call(
        _out_kernel,
        grid=(b, n),
        in_specs=[pl.BlockSpec((1, N_PAIRS, tile, LANES), lambda i, j: (i, 0, j, 0)),
                  tok(d), _const_spec((1, d)), _const_spec(w_z.shape),
                  _const_spec(w_out.shape), _const_spec((1, d))],
        out_specs=tok(d),
        out_shape=jax.ShapeDtypeStruct((b, s, d), F32),
        compiler_params=pltpu.CompilerParams(
            dimension_semantics=("arbitrary", "arbitrary"),
            vmem_limit_bytes=VMEM_LIMIT),
        name="out_layer",
    )(o, x, g1, w_z, w_out, g)


def _rope_block_cols(w):
    half = QK_ROPE_DIM // 2
    fill = jnp.zeros(w.shape[:-1] + (HEAD_BLOCK - ROPE_HI - half,), w.dtype)
    return jnp.concatenate([w, w[..., :half], fill], axis=-1)


def _prep_weights(mla_w_in, mla_w_qb, mla_w_kvb):
    o_kv = Q_LORA_RANK
    o_kr = o_kv + KV_LORA_RANK
    o_z = o_kr + QK_ROPE_DIM
    d = mla_w_in.shape[0]
    kr_cols = jnp.concatenate(
        [jnp.zeros((d, ROPE_LO), F32), _rope_block_cols(mla_w_in[:, o_kr:o_z])], axis=1)
    w_in2 = jnp.concatenate([mla_w_in[:, :o_kr], kr_cols], axis=1)
    w_z = mla_w_in[:, o_z:]

    wqb = mla_w_qb.reshape(Q_LORA_RANK, N_HEADS, QK_DIM)
    half = QK_ROPE_DIM // 2
    x1, x2 = wqb[..., QK_NOPE_DIM:QK_NOPE_DIM + half], wqb[..., QK_NOPE_DIM + half:]
    lane_pad = ((0, 0), (0, 0), (0, HEAD_BLOCK - QK_DIM))
    wqb_main = jnp.pad(wqb, lane_pad)
    wqb_partner = jnp.pad(jnp.concatenate([jnp.zeros_like(wqb[..., :QK_NOPE_DIM]), x2, x1], axis=-1),
                          lane_pad)
    wqb = jnp.concatenate([wqb_main.reshape(Q_LORA_RANK, N_HEADS * HEAD_BLOCK),
                           wqb_partner.reshape(Q_LORA_RANK, N_HEADS * HEAD_BLOCK)], axis=1)

    wkvb = mla_w_kvb.reshape(KV_LORA_RANK, N_HEADS, QK_NOPE_DIM + V_HEAD_DIM)
    wk_pad = jnp.pad(wkvb[:, :, :QK_NOPE_DIM], ((0, 0), (0, 0), (0, HEAD_BLOCK - QK_NOPE_DIM)))
    place = np.zeros((HEAD_BLOCK, N_HEADS, HEAD_BLOCK), np.float32)
    lanes = np.arange(ROPE_LO, ROPE_HI)
    place[lanes, :, lanes] = 1.0
    wk = jnp.concatenate([wk_pad, place], axis=0).reshape(KV_LORA_RANK + HEAD_BLOCK,
                                                          N_HEADS * HEAD_BLOCK)
    wv = wkvb[:, :, QK_NOPE_DIM:]
    wvt = jnp.pad(jnp.transpose(wv, (1, 2, 0)), ((0, 0), (0, VT_ROWS - V_HEAD_DIM), (0, 0)))
    wvt = wvt.reshape(N_HEADS * VT_ROWS, KV_LORA_RANK)

    wkt = jnp.transpose(wk_pad, (1, 2, 0))
    wv_lo = jnp.pad(wv, ((0, 0), (0, 0), (0, LANES - V_HEAD_DIM)))
    wv_hi = jnp.pad(wv, ((0, 0), (0, 0), (LANES - V_HEAD_DIM, 0)))
    odd = (jnp.arange(N_HEADS) % 2 == 1)[None, :, None]
    wvp = jnp.transpose(jnp.where(odd, wv_hi, wv_lo), (1, 0, 2))
    return tuple(w.astype(BF16) for w in (w_in2, w_z, wqb, wk, wvt, wkt, wvp))


def _inv_freq_lanes():
    inv = 1.0 / (ROPE_THETA ** (jnp.arange(0, QK_ROPE_DIM, 2, dtype=F32) / QK_ROPE_DIM))
    z = lambda n: jnp.zeros((n,), F32)
    return jnp.concatenate([z(ROPE_LO), inv, inv, z(HEAD_BLOCK - ROPE_HI)])[None, :]


def _rope_placement():
    place = np.zeros((HEAD_BLOCK, QK_ROPE_DIM), np.float32)
    rows = np.arange(QK_ROPE_DIM)
    place[ROPE_LO + rows, rows] = 1.0
    return jnp.asarray(place, BF16)


def kernel(x_prompt, x_sample, state_conv, cache_ckv, cache_krope, norm_g, final_norm_g,
           conv_w_in, conv_w, conv_w_out, mla_w_in, mla_q_norm_g, mla_w_qb,
           mla_kv_norm_g, mla_w_kvb, mla_w_out):
    bp, s, d = x_prompt.shape
    bs, t, _ = x_sample.shape
    past = cache_ckv.shape[2]

    tok_tile = 512
    attn_tile = 512
    cache_tile = 1024
    group = 4

    g0 = norm_g[0][None, :]
    g1 = norm_g[1][None, :]
    gf = final_norm_g[None, :]
    cw_in = conv_w_in[0].astype(BF16)
    cw_out = conv_w_out[0].astype(BF16)
    w_in2, w_z, wqb, wk, wvt, wkt, wvp = _prep_weights(mla_w_in[0], mla_w_qb[0], mla_w_kvb[0])
    qg = mla_q_norm_g[0][None, :]
    kvg = mla_kv_norm_g[0][None, :]
    w_out = mla_w_out[0].astype(BF16)
    inv_lane = _inv_freq_lanes()

    zeros_prev = jnp.zeros((bp, CONV_WIDTH - 1, d), F32)
    xp1, st_p = _conv_layer(x_prompt, zeros_prev, g0, cw_in, conv_w[0], cw_out, tok_tile)
    xs1, st_s = _conv_layer(x_sample, state_conv[0], g0, cw_in, conv_w[0], cw_out, t)

    cos_p, sin_p = _rope_tables(inv_lane, s, 0, 512)
    cos_s, sin_s = _rope_tables(inv_lane, t, past, t)

    qp, ckv_p, krt_p, kp, vp = _mla_project(
        xp1, g1, w_in2, qg, wqb, kvg, wk, wvt, cos_p, sin_p, attn_tile, True)
    xs1_flat = xs1.reshape(1, bs * t, d)
    qs, ckv_s, krt_s = _mla_project(
        xs1_flat, g1, w_in2, qg, wqb, kvg, wk, wvt,
        jnp.tile(cos_s, (bs, 1)), jnp.tile(sin_s, (bs, 1)), bs * t, False)
    krt_s = jnp.swapaxes(krt_s.reshape(QK_ROPE_DIM, bs, t), 0, 1)

    op = _attention(qp, kp, vp, group)
    cache_krt = jnp.swapaxes(cache_krope[0], 1, 2)
    os_ = _latent_attention(qs, cache_ckv[0], cache_krt, ckv_s, krt_s,
                            _rope_placement(), wkt, wvp, cache_tile)

    y_prompt = _out_layer(op, xp1, g1, w_z, w_out, gf, tok_tile)
    y_sample = _out_layer(os_, xs1_flat, g1, w_z, w_out, gf, bs * t).reshape(bs, t, d)

    return (y_prompt, y_sample, st_p[None], ckv_p[None], jnp.swapaxes(krt_p, 1, 2)[None],
            st_s[None], ckv_s.reshape(1, bs, t, KV_LORA_RANK), jnp.swapaxes(krt_s, 1, 2)[None])
```

```python
import functools
import math

import jax
import jax.numpy as jnp
import numpy as np
from jax import lax
from jax.experimental import pallas as pl
from jax.experimental.pallas import tpu as pltpu

D_MODEL = 1024
CHUNK = 64
CONV_WIDTH = 3
N_HEADS = 16
QK_NOPE_DIM = 64
QK_ROPE_DIM = 32
V_HEAD_DIM = 64
QK_DIM = QK_NOPE_DIM + QK_ROPE_DIM
Q_LORA_RANK = 256
KV_LORA_RANK = 128
MLA_GATE_DIM = N_HEADS * V_HEAD_DIM
ROPE_THETA = 10000.0
EPS = 1e-6
NEG_INF = -1e30
PAST_LEN = 4096

LANES = 128
HEAD_BLOCK = LANES
N_PAIRS = N_HEADS // 2
VT_ROWS = V_HEAD_DIM + 16
ROPE_LO = QK_NOPE_DIM
ROPE_MID = ROPE_LO + QK_ROPE_DIM // 2
ROPE_HI = ROPE_LO + QK_ROPE_DIM
Q_SCALE = (QK_DIM ** -0.5) * math.log2(math.e)

VMEM_LIMIT =56 * 1024 * 1024
F32 = jnp.float32
BF16 = jnp.bfloat16


def _rms(x, g):
    ms = jnp.mean(x * x, axis=-1, keepdims=True)
    return x * lax.rsqrt(ms + EPS) * g


def _const_spec(shape):
    nd = len(shape)
    return pl.BlockSpec(shape, lambda *_: (0,) * nd)


def _conv_kernel(x_ref, prev_ref, g_ref, win_ref, cw_ref, wout_ref, xo_ref, st_ref, ubuf):
    j = pl.program_id(1)
    n_seq = ubuf.shape[0]
    t = x_ref.shape[1] // n_seq
    halo = CONV_WIDTH - 1
    base = 8

    @pl.when(j == 0)
    def _():
        for s in range(n_seq):
            ubuf[s, base - halo:base, :] = prev_ref[s]

    x = x_ref[0]
    hb = _rms(x, g_ref[...]).astype(BF16)

    def proj(k):
        return jnp.dot(hb, win_ref[:, k * D_MODEL:(k + 1) * D_MODEL],
                       preferred_element_type=F32)

    u = proj(1) * proj(2)
    convs, new_halos = [], []
    for s in range(n_seq):
        us = u[s * t:(s + 1) * t, :]
        ubuf[s, base:base + t, :] = us
        convs.append(ubuf[s, base - 2:base - 2 + t, :] * cw_ref[0:1, :]
                     + ubuf[s, base - 1:base - 1 + t, :] * cw_ref[1:2, :]
                     + us * cw_ref[2:3, :])
        new_halos.append(ubuf[s, base + t - halo:base + t, :])
        ubuf[s, base - halo:base, :] = new_halos[s]
    conv = convs[0] if n_seq == 1 else jnp.concatenate(convs, axis=0)
    z = proj(3)
    gated = (z * jax.nn.sigmoid(z)) * proj(0) * conv
    y = jnp.dot(gated.astype(BF16), wout_ref[...], preferred_element_type=F32)
    xo_ref[0] = x + y

    @pl.when(j == pl.num_programs(1) - 1)
    def _():
        for s in range(n_seq):
            st_ref[s] = new_halos[s]


def _conv_layer(x, prev, g, w_in, conv_w, w_out, tile):
    b, s, d = x.shape
    n = s // tile
    n_seq = prev.shape[0] // b
    assert n_seq == 1 or (n == 1 and tile % (8 * n_seq) == 0)
    return pl.pallas_call(
        _conv_kernel,
        grid=(b, n),
        in_specs=[
            pl.BlockSpec((1, tile, d), lambda i, j: (i, j, 0)),
            pl.BlockSpec((n_seq, CONV_WIDTH - 1, d), lambda i, j: (i, 0, 0)),
            _const_spec((1, d)),
            _const_spec(w_in.shape),
            _const_spec(conv_w.shape),
            _const_spec(w_out.shape),
        ],
        out_specs=[
            pl.BlockSpec((1, tile, d), lambda i, j: (i, j, 0)),
            pl.BlockSpec((n_seq, CONV_WIDTH - 1, d), lambda i, j: (i, 0, 0)),
        ],
        out_shape=[
            jax.ShapeDtypeStruct((b, s, d), F32),
            jax.ShapeDtypeStruct((b * n_seq, CONV_WIDTH - 1, d), F32),
        ],
        scratch_shapes=[pltpu.VMEM((n_seq, tile // n_seq + 8, d), F32)],
        compiler_params=pltpu.CompilerParams(
            dimension_semantics=("arbitrary", "arbitrary"),
            vmem_limit_bytes=VMEM_LIMIT),
        name="conv_layer",
    )(x, prev, g, w_in, conv_w, w_out)


def _rope_table_kernel(inv_ref, cos_ref, sin_ref, cos_row, sin_row, *, offset):
    t = cos_ref.shape[0]
    j = pl.program_id(0)
    inv = inv_ref[...]

    @pl.when(j == 0)
    def _():
        row = lax.broadcasted_iota(jnp.int32, (t, LANES), 0).astype(F32)
        cos_row[...] = jnp.cos(row * inv)
        sin_row[...] = jnp.sin(row * inv)

    start = jnp.full((8, LANES), (offset + j * t).astype(F32)) * inv
    cos_start, sin_start = jnp.cos(start)[0:1, :], jnp.sin(start)[0:1, :]
    lane = lax.broadcasted_iota(jnp.int32, (t, LANES), 1)
    cos_ref[...] = cos_start * cos_row[...] - sin_start * sin_row[...]
    s = sin_start * cos_row[...] + cos_start * sin_row[...]
    sin_ref[...] = jnp.where(lane < ROPE_MID, -s, s)


def _rope_tables(inv_lane, s, offset, tile):
    return pl.pallas_call(
        functools.partial(_rope_table_kernel, offset=offset),
        grid=(s // tile,),
        in_specs=[_const_spec((1, LANES))],
        out_specs=[pl.BlockSpec((tile, LANES), lambda i: (i, 0))] * 2,
        out_shape=[jax.ShapeDtypeStruct((s, LANES), F32)] * 2,
        scratch_shapes=[pltpu.VMEM((tile, LANES), F32)] * 2,
        compiler_params=pltpu.CompilerParams(dimension_semantics=("arbitrary",)),
        name="rope_tables",
    )(inv_lane)


def _rope(v, cos, sin_signed):
    partner = pltpu.roll(v, LANES - QK_ROPE_DIM // 2, 1)
    return v * cos + partner * sin_signed


def _expand_kv(ckv, kr_block, wk_ref, wvt_ref, k_ref, vt_ref):
    t = ckv.shape[0]
    k_in = jnp.concatenate([ckv.astype(BF16), kr_block.astype(BF16)], axis=1)
    k_all = jnp.dot(k_in, wk_ref[...], preferred_element_type=F32)
    for h in range(N_HEADS):
        k_ref[0, h] = k_all[:, h * HEAD_BLOCK:(h + 1) * HEAD_BLOCK].astype(BF16)
    vt = jnp.dot(wvt_ref[...], ckv.T.astype(BF16), preferred_element_type=F32)
    ones_row = (lax.broadcasted_iota(jnp.int32, (VT_ROWS, t), 0) == V_HEAD_DIM).astype(F32)
    for h in range(N_HEADS):
        vt_ref[0, h, 0] = (vt[h * VT_ROWS:(h + 1) * VT_ROWS, :] + ones_row).astype(BF16)


def _vt_spec(tile):
    return pl.BlockSpec((1, N_HEADS, 1, VT_ROWS, tile), lambda i, j: (i, 0, j, 0, 0))


def _mla_project_kernel(x_ref, g_ref, win_ref, qg_ref, wqb_ref, kvg_ref, wk_ref, wvt_ref,
                        cos_ref, sin_ref, q_ref, ckv_ref, kr_ref, *kv_refs):
    hb = _rms(x_ref[0], g_ref[...]).astype(BF16)
    o_kv = Q_LORA_RANK
    o_kr = o_kv + KV_LORA_RANK
    o_z = o_kr + HEAD_BLOCK
    p = jnp.dot(hb, win_ref[...], preferred_element_type=F32)

    ckv = _rms(p[:, o_kv:o_kr], kvg_ref[...])
    ckv_ref[0] = ckv

    cos = cos_ref[...]
    sin_signed = sin_ref[...]
    kr_block = _rope(p[:, o_kr:o_z], cos, sin_signed)
    kr_ref[0] = kr_block.T[ROPE_LO:ROPE_HI, :]

    qn = _rms(p[:, 0:o_kv], qg_ref[...]).astype(BF16)
    q2 = jnp.dot(qn, wqb_ref[...], preferred_element_type=F32)
    q, q_partner = q2[:, :N_HEADS * HEAD_BLOCK], q2[:, N_HEADS * HEAD_BLOCK:]
    cos_q = cos * Q_SCALE
    sin_q = sin_signed * Q_SCALE
    for h in range(N_HEADS):
        hs = slice(h * HEAD_BLOCK, (h + 1) * HEAD_BLOCK)
        q_ref[0, h] = (q[:, hs] * cos_q + q_partner[:, hs] * sin_q).astype(BF16)

    if kv_refs:
        _expand_kv(ckv, kr_block, wk_ref, wvt_ref, *kv_refs)


def _mla_project(x, g, w_in2, qg, wqb, kvg, wk, wvt, cos, sin_signed, tile, emit_kv):
    b, s, d = x.shape
    n = s // tile
    tok = lambda w: pl.BlockSpec((1, tile, w), lambda i, j: (i, j, 0))
    head = lambda nh: pl.BlockSpec((1, nh, tile, LANES), lambda i, j: (i, 0, j, 0))
    out_specs = [head(N_HEADS), tok(KV_LORA_RANK),
                 pl.BlockSpec((1, QK_ROPE_DIM, tile), lambda i, j: (i, 0, j))]
    out_shape = [
        jax.ShapeDtypeStruct((b, N_HEADS, s, HEAD_BLOCK), BF16),
        jax.ShapeDtypeStruct((b, s, KV_LORA_RANK), F32),
        jax.ShapeDtypeStruct((b, QK_ROPE_DIM, s), F32),
    ]
    if emit_kv:
        out_specs += [head(N_HEADS), _vt_spec(tile)]
        out_shape += [jax.ShapeDtypeStruct((b, N_HEADS, s, HEAD_BLOCK), BF16),
                      jax.ShapeDtypeStruct((b, N_HEADS, n, VT_ROWS, tile), BF16)]
    return pl.pallas_call(
        _mla_project_kernel,
        grid=(b, n),
        in_specs=[
            tok(d),
            _const_spec((1, d)),
            _const_spec(w_in2.shape),
            _const_spec(qg.shape),
            _const_spec(wqb.shape),
            _const_spec(kvg.shape),
            _const_spec(wk.shape),
            _const_spec(wvt.shape),
            pl.BlockSpec((tile, LANES), lambda i, j: (j, 0)),
            pl.BlockSpec((tile, LANES), lambda i, j: (j, 0)),
        ],
        out_specs=out_specs,
        out_shape=out_shape,
        compiler_params=pltpu.CompilerParams(
            dimension_semantics=("arbitrary", "arbitrary"),
            vmem_limit_bytes=VMEM_LIMIT),
        name="mla_project",
    )(x, g, w_in2, qg, wqb, kvg, wk, wvt, cos, sin_signed)


def _latent_attn_kernel(q_ref, ckv_ref, kr_ref, ckvn_ref, krn_ref, place_ref, wkt_ref, wvp_ref,
                        o_ref, qa_sc, m_sc, l_sc, acc_sc, st_sc):
    j = pl.program_id(1)
    t = q_ref.shape[2]
    cols = N_HEADS * t
    col_tile = 2 * LANES

    @pl.when(j == 0)
    def _():
        lane = lax.broadcasted_iota(jnp.int32, (t, LANES), 1)
        is_rope = (lane >= ROPE_LO) & (lane < ROPE_HI)
        for h in range(N_HEADS):
            qh = q_ref[0, h]
            q_lat = jnp.dot(qh, wkt_ref[h], preferred_element_type=F32)
            qa_sc[h * t:(h + 1) * t, 0:LANES] = q_lat.astype(BF16)
            qa_sc[h * t:(h + 1) * t, LANES:2 * LANES] = jnp.where(is_rope, qh, jnp.zeros_like(qh))
        m_sc[...] = jnp.full(m_sc.shape, NEG_INF, F32)
        l_sc[...] = jnp.zeros(l_sc.shape, F32)
        acc_sc[...] = jnp.zeros(acc_sc.shape, F32)

    def attend(ckv, kr_t):
        tk = ckv.shape[0]
        kr_block = jnp.dot(place_ref[...], kr_t.astype(BF16), preferred_element_type=F32).T
        keys = jnp.concatenate([ckv.astype(BF16), kr_block.astype(BF16)], axis=1)
        ones_rows = (lax.broadcasted_iota(jnp.int32, (16, tk), 0) == 0).astype(F32)
        vals_t = jnp.concatenate([ckv.T, ones_rows], axis=0).astype(BF16)
        n_col = cols // col_tile
        col = lambda c: slice(c * col_tile, (c + 1) * col_tile)
        scores = lambda c: lax.dot_general(keys, qa_sc[col(c), :], (((1,), (1,)), ((), ())),
                                           preferred_element_type=F32)
        col_max = []
        for c in range(n_col):
            st = scores(c)
            st_sc[c, 0:tk, :] = st
            col_max.append(jnp.max(st, axis=0, keepdims=True))
        for c in range(n_col):
            cs = col(c)
            m_prev = m_sc[:, cs]
            m_new = jnp.maximum(m_prev, col_max[c])
            alpha = jnp.exp2(m_prev - m_new)
            pt = jnp.exp2(st_sc[c, 0:tk, :] - m_new).astype(BF16)
            pv = jnp.dot(vals_t, pt, preferred_element_type=F32)
            l_sc[:, cs] = alpha * l_sc[:, cs] + pv[KV_LORA_RANK:KV_LORA_RANK + 1, :]
            m_sc[:, cs] = m_new
            acc_sc[:, cs] = acc_sc[:, cs] * alpha + pv[0:KV_LORA_RANK, :]

    attend(ckv_ref[0], kr_ref[0])

    @pl.when(j == pl.num_programs(1) - 1)
    def _():
        attend(ckvn_ref[0], krn_ref[0])
        o_lat = (acc_sc[...] / l_sc[...]).T.astype(BF16)
        for p in range(N_PAIRS):
            pair = sum(jnp.dot(o_lat[h * t:(h + 1) * t, :], wvp_ref[h],
                               preferred_element_type=F32) for h in (2 * p, 2 * p + 1))
            o_ref[0, p] = pair.astype(o_ref.dtype)


def _latent_attention(q, cache_ckv, cache_krt, ckv_new, krt_new, place, wkt, wvp, tile):
    b, past = cache_ckv.shape[0], cache_ckv.shape[1]
    nh, t = q.shape[1], q.shape[2] // b
    cols = nh * t
    assert past % tile == 0 and cols % (2 * LANES) == 0
    return pl.pallas_call(
        _latent_attn_kernel,
        grid=(b, past // tile),
        in_specs=[
            pl.BlockSpec((1, nh, t, HEAD_BLOCK), lambda i, j: (0, 0, i, 0)),
            pl.BlockSpec((1, tile, KV_LORA_RANK), lambda i, j: (i, j, 0)),
            pl.BlockSpec((1, QK_ROPE_DIM, tile), lambda i, j: (i, 0, j)),
            pl.BlockSpec((1, t, KV_LORA_RANK), lambda i, j: (0, i, 0)),
            pl.BlockSpec((1, QK_ROPE_DIM, t), lambda i, j: (i, 0, 0)),
            _const_spec(place.shape), _const_spec(wkt.shape), _const_spec(wvp.shape),
        ],
        out_specs=pl.BlockSpec((1, N_PAIRS, t, LANES), lambda i, j: (0, 0, i, 0)),
        out_shape=jax.ShapeDtypeStruct((1, N_PAIRS, b * t, LANES), BF16),
        scratch_shapes=[pltpu.VMEM((cols, 2 * LANES), BF16),
                        pltpu.VMEM((1, cols), F32),
                        pltpu.VMEM((1, cols), F32),
                        pltpu.VMEM((KV_LORA_RANK, cols), F32),
                        pltpu.VMEM((cols // (2 * LANES), tile, 2 * LANES), F32)],
        compiler_params=pltpu.CompilerParams(
            dimension_semantics=("arbitrary", "arbitrary"),
            vmem_limit_bytes=VMEM_LIMIT),
        name="latent_attention",
    )(q, cache_ckv, cache_krt, ckv_new, krt_new, place, wkt, wvp)


def _attn_kernel(q_ref, qn_ref, k_ref, vt_ref, o_ref, m_sc, l_sc, acc_sc, s_sc, mx_sc, d_sc):
    n_group, tq = q_ref.shape[1], q_ref.shape[2]
    tk = vt_ref.shape[4]
    i = pl.program_id(2)
    last_q = pl.num_programs(2) - 1

    m_sc[...] = jnp.full(m_sc.shape, NEG_INF, F32)
    l_sc[...] = jnp.zeros(l_sc.shape, F32)
    acc_sc[...] = jnp.zeros(acc_sc.shape, F32)

    def scores(qr, h, j):
        k = k_ref[0, h, pl.ds(pl.multiple_of(j * tk, tk), tk), :]
        return lax.dot_general(k, qr[0, h], (((1,), (1,)), ((), ())),
                               preferred_element_type=F32)

    def issue_scores(slot, h, j):
        st = scores(q_ref, h, j)
        s_sc[slot, h] = st
        mx_sc[slot, h] = jnp.max(st, axis=0, keepdims=True)

    def issue_next_diagonal(h):
        d_sc[h] = scores(qn_ref, h, jnp.minimum(i + 1, last_q))

    def pv_dot(vt, st, m):
        return jnp.dot(vt, jnp.exp2(st - m).astype(BF16), preferred_element_type=F32)

    def online_update(h, col_max, pv_of):
        pr, lo = h // 2, (h % 2) * V_HEAD_DIM
        m_prev = m_sc[h]
        m_new = jnp.maximum(m_prev, col_max)
        alpha = jnp.exp2(m_prev - m_new)
        pv = pv_of(m_new)
        l_sc[h] = alpha * l_sc[h] + pv[V_HEAD_DIM:V_HEAD_DIM + 1, :]
        m_sc[h] = m_new
        acc_sc[pr, lo:lo + V_HEAD_DIM, :] = (acc_sc[pr, lo:lo + V_HEAD_DIM, :] * alpha
                                             + pv[0:V_HEAD_DIM, :])

    def tile(j, slot, issue):
        for h in range(n_group):
            issue(h)
            online_update(h, mx_sc[slot, h],
                          lambda m, h=h: pv_dot(vt_ref[0, h, j], s_sc[slot, h], m))

    hk, hq = tk // 2, tq // 2
    quadrant_mask = (lax.broadcasted_iota(jnp.int32, (hk, hq), 0) // CHUNK
                     <= lax.broadcasted_iota(jnp.int32, (hk, hq), 1) // CHUNK)

    def diagonal_tile(h):
        mask = quadrant_mask
        st_aa = jnp.where(mask, d_sc[h, 0:hk, 0:hq], NEG_INF)
        st_ab = d_sc[h, 0:hk, hq:tq]
        st_bb = jnp.where(mask, d_sc[h, hk:tk, hq:tq], NEG_INF)
        col_max = jnp.concatenate(
            [jnp.max(st_aa, axis=0, keepdims=True),
             jnp.maximum(jnp.max(st_ab, axis=0, keepdims=True),
                         jnp.max(st_bb, axis=0, keepdims=True))], axis=1)
        vt_a = vt_ref[0, h, i, :, 0:hk]
        vt_b = vt_ref[0, h, i, :, hk:tk]
        online_update(h, col_max, lambda m: jnp.concatenate(
            [pv_dot(vt_a, st_aa, m[:, 0:hq]),
             pv_dot(vt_a, st_ab, m[:, hq:tq]) + pv_dot(vt_b, st_bb, m[:, hq:tq])], axis=1))

    @pl.when(i == 0)
    def _():
        for h in range(n_group):
            d_sc[h] = scores(q_ref, h, 0)

    for h in range(n_group):
        issue_scores(0, h, 0)
        diagonal_tile(h)

    def tiles_from(j0, count):
        for d in range(count):
            tile(j0 + d, d % 2, lambda h, d=d: issue_scores((d + 1) % 2, h, j0 + d + 1))

    def body4(jj, carry):
        tiles_from(4 * jj, 4)
        return carry

    def body2(jj, carry):
        tiles_from(2 * jj, 2)
        return carry

    def body1(j, carry):
        tiles_from(j, 1)
        return carry

    n_loop = jnp.maximum(i - 1, 0)
    n_quads, n_pairs = n_loop // 4, n_loop // 2
    lax.fori_loop(0, n_quads, body4, 0)
    lax.fori_loop(2 * n_quads, n_pairs, body2, 0)
    lax.fori_loop(2 * n_pairs, n_loop, body1, 0)

    def write_output():
        for pr in range(n_group // 2):
            inv_l = jnp.concatenate(
                [jnp.broadcast_to(1.0 / l_sc[2 * pr + hh], (V_HEAD_DIM, tq)) for hh in range(2)],
                axis=0)
            o_ref[0, pr] = (acc_sc[pr] * inv_l).T.astype(o_ref.dtype)

    @pl.when(i > 0)
    def _():
        tile(i - 1, (i - 1) % 2, issue_next_diagonal)
        write_output()

    @pl.when(i == 0)
    def _():
        for h in range(n_group):
            issue_next_diagonal(h)
        write_output()


def _attention(q, k, vt, group):
    b, nh, sq, _ = q.shape
    skv = k.shape[2]
    n_kv, tk = vt.shape[2], vt.shape[4]
    tq = tk
    assert n_kv * tk == skv == sq and (tk // 2) % CHUNK == 0
    gp = group // 2
    n_q = sq // tq
    return pl.pallas_call(
        _attn_kernel,
        grid=(b, nh // group, n_q),
        in_specs=[
            pl.BlockSpec((1, group, tq, HEAD_BLOCK), lambda i, g, j: (i, g, j, 0)),
            pl.BlockSpec((1, group, tq, HEAD_BLOCK),
                         lambda i, g, j: (i, g, jnp.minimum(j + 1, n_q - 1), 0)),
            pl.BlockSpec((1, group, skv, HEAD_BLOCK), lambda i, g, j: (i, g, 0, 0)),
            pl.BlockSpec((1, group, n_kv, VT_ROWS, tk), lambda i, g, j: (i, g, 0, 0, 0)),
        ],
        out_specs=pl.BlockSpec((1, gp, tq, LANES), lambda i, g, j: (i, g, j, 0)),
        out_shape=jax.ShapeDtypeStruct((b, nh // 2, sq, LANES), BF16),
        scratch_shapes=[pltpu.VMEM((group, 1, tq), F32),
                        pltpu.VMEM((group, 1, tq), F32),
                        pltpu.VMEM((gp, LANES, tq), F32),
                        pltpu.VMEM((2, group, tk, tq), F32),
                        pltpu.VMEM((2, group, 1, tq), F32),
                        pltpu.VMEM((group, tk, tq), F32)],
        compiler_params=pltpu.CompilerParams(
            dimension_semantics=("arbitrary", "arbitrary", "arbitrary"),
            vmem_limit_bytes=VMEM_LIMIT),
        name="attention",
    )(q, q, k, vt)


def _out_kernel(o_ref, x_ref, g1_ref, wz_ref, w_ref, g_ref, y_ref):
    t = x_ref.shape[1]
    parts = 2 if t >= 4 * LANES else 1
    rows = [slice(p * t // parts, (p + 1) * t // parts) for p in range(parts)]
    xs = [x_ref[0, r, :] for r in rows]
    zs = [jnp.dot(_rms(x, g1_ref[...]).astype(BF16), wz_ref[...], preferred_element_type=F32)
          for x in xs]
    for r, x, z in zip(rows, xs, zs):
        o = jnp.concatenate([o_ref[0, p, r, :] for p in range(N_PAIRS)], axis=1)
        gated = (z * jax.nn.sigmoid(z) * o.astype(F32)).astype(BF16)
        x2 = x + jnp.dot(gated, w_ref[...], preferred_element_type=F32)
        y_ref[0, r, :] = _rms(x2, g_ref[...])


def _out_layer(o, x, g1, w_z, w_out, g, tile):
    b, s, d = x.shape
    n = s // tile
    tok = lambda w: pl.BlockSpec((1, tile, w), lambda i, j: (i, j, 0))
    return pl.pallas_call(
        _out_kernel,
        grid=(b, n),
        in_specs=[pl.BlockSpec((1, N_PAIRS, tile, LANES), lambda i, j: (i, 0, j, 0)),
                  tok(d), _const_spec((1, d)), _const_spec(w_z.shape),
                  _const_spec(w_out.shape), _const_spec((1, d))],
        out_specs=tok(d),
        out_shape=jax.ShapeDtypeStruct((b, s, d), F32),
        compiler_params=pltpu.CompilerParams(
            dimension_semantics=("arbitrary", "arbitrary"),
            vmem_limit_bytes=VMEM_LIMIT),
        name="out_layer",
    )(o, x, g1, w_z, w_out, g)


def _rope_block_cols(w):
    half = QK_ROPE_DIM // 2
    fill = jnp.zeros(w.shape[:-1] + (HEAD_BLOCK - ROPE_HI - half,), w.dtype)
    return jnp.concatenate([w, w[..., :half], fill], axis=-1)


def _prep_weights(mla_w_in, mla_w_qb, mla_w_kvb):
    o_kv = Q_LORA_RANK
    o_kr = o_kv + KV_LORA_RANK
    o_z = o_kr + QK_ROPE_DIM
    d = mla_w_in.shape[0]
    kr_cols = jnp.concatenate(
        [jnp.zeros((d, ROPE_LO), F32), _rope_block_cols(mla_w_in[:, o_kr:o_z])], axis=1)
    w_in2 = jnp.concatenate([mla_w_in[:, :o_kr], kr_cols], axis=1)
    w_z = mla_w_in[:, o_z:]

    wqb = mla_w_qb.reshape(Q_LORA_RANK, N_HEADS, QK_DIM)
    half = QK_ROPE_DIM // 2
    x1, x2 = wqb[..., QK_NOPE_DIM:QK_NOPE_DIM + half], wqb[..., QK_NOPE_DIM + half:]
    lane_pad = ((0, 0), (0, 0), (0, HEAD_BLOCK - QK_DIM))
    wqb_main = jnp.pad(wqb, lane_pad)
    wqb_partner = jnp.pad(jnp.concatenate([jnp.zeros_like(wqb[..., :QK_NOPE_DIM]), x2, x1], axis=-1),
                          lane_pad)
    wqb = jnp.concatenate([wqb_main.reshape(Q_LORA_RANK, N_HEADS * HEAD_BLOCK),
                           wqb_partner.reshape(Q_LORA_RANK, N_HEADS * HEAD_BLOCK)], axis=1)

    wkvb = mla_w_kvb.reshape(KV_LORA_RANK, N_HEADS, QK_NOPE_DIM + V_HEAD_DIM)
    wk_pad = jnp.pad(wkvb[:, :, :QK_NOPE_DIM], ((0, 0), (0, 0), (0, HEAD_BLOCK - QK_NOPE_DIM)))
    place = np.zeros((HEAD_BLOCK, N_HEADS, HEAD_BLOCK), np.float32)
    lanes = np.arange(ROPE_LO, ROPE_HI)
    place[lanes, :, lanes] = 1.0
    wk = jnp.concatenate([wk_pad, place], axis=0).reshape(KV_LORA_RANK + HEAD_BLOCK,
                                                          N_HEADS * HEAD_BLOCK)
    wv = wkvb[:, :, QK_NOPE_DIM:]
    wvt = jnp.pad(jnp.transpose(wv, (1, 2, 0)), ((0, 0), (0, VT_ROWS - V_HEAD_DIM), (0, 0)))
    wvt = wvt.reshape(N_HEADS * VT_ROWS, KV_LORA_RANK)

    wkt = jnp.transpose(wk_pad, (1, 2, 0))
    wv_lo = jnp.pad(wv, ((0, 0), (0, 0), (0, LANES - V_HEAD_DIM)))
    wv_hi = jnp.pad(wv, ((0, 0), (0, 0), (LANES - V_HEAD_DIM, 0)))
    odd = (jnp.arange(N_HEADS) % 2 == 1)[None, :, None]
    wvp = jnp.transpose(jnp.where(odd, wv_hi, wv_lo), (1, 0, 2))
    return tuple(w.astype(BF16) for w in (w_in2, w_z, wqb, wk, wvt, wkt, wvp))


def _inv_freq_lanes():
    inv = 1.0 / (ROPE_THETA ** (jnp.arange(0, QK_ROPE_DIM, 2, dtype=F32) / QK_ROPE_DIM))
    z = lambda n: jnp.zeros((n,), F32)
    return jnp.concatenate([z(ROPE_LO), inv, inv, z(HEAD_BLOCK - ROPE_HI)])[None, :]


def _rope_placement():
    place = np.zeros((HEAD_BLOCK, QK_ROPE_DIM), np.float32)
    rows = np.arange(QK_ROPE_DIM)
    place[ROPE_LO + rows, rows] = 1.0
    return jnp.asarray(place, BF16)


def kernel(x_prompt, x_sample, state_conv, cache_ckv, cache_krope, norm_g, final_norm_g,
           conv_w_in, conv_w, conv_w_out, mla_w_in, mla_q_norm_g, mla_w_qb,
           mla_kv_norm_g, mla_w_kvb, mla_w_out):
    bp, s, d = x_prompt.shape
    bs, t, _ = x_sample.shape
    past = cache_ckv.shape[2]

    tok_tile = 512
    attn_tile = 512
    cache_tile = 1024
    group = 4

    g0 = norm_g[0][None, :]
    g1 = norm_g[1][None, :]
    gf = final_norm_g[None, :]
    cw_in = conv_w_in[0].astype(BF16)
    cw_out = conv_w_out[0].astype(BF16)
    w_in2, w_z, wqb, wk, wvt, wkt, wvp = _prep_weights(mla_w_in[0], mla_w_qb[0], mla_w_kvb[0])
    qg = mla_q_norm_g[0][None, :]
    kvg = mla_kv_norm_g[0][None, :]
    w_out = mla_w_out[0].astype(BF16)
    inv_lane = _inv_freq_lanes()

    zeros_prev = jnp.zeros((bp, CONV_WIDTH - 1, d), F32)
    xp1, st_p = _conv_layer(x_prompt, zeros_prev, g0, cw_in, conv_w[0], cw_out, tok_tile)
    xs1_flat, st_s = _conv_layer(x_sample.reshape(1, bs * t, d), state_conv[0], g0,
                                 cw_in, conv_w[0], cw_out, bs * t)

    cos_p, sin_p = _rope_tables(inv_lane, s, 0, 512)
    cos_s, sin_s = _rope_tables(inv_lane, t, past, t)

    qp, ckv_p, krt_p, kp, vp = _mla_project(
        xp1, g1, w_in2, qg, wqb, kvg, wk, wvt, cos_p, sin_p, attn_tile, True)
    qs, ckv_s, krt_s = _mla_project(
        xs1_flat, g1, w_in2, qg, wqb, kvg, wk, wvt,
        jnp.tile(cos_s, (bs, 1)), jnp.tile(sin_s, (bs, 1)), bs * t, False)
    krt_s = jnp.swapaxes(krt_s.reshape(QK_ROPE_DIM, bs, t), 0, 1)

    op = _attention(qp, kp, vp, group)
    cache_krt = jnp.swapaxes(cache_krope[0], 1, 2)
    os_ = _latent_attention(qs, cache_ckv[0], cache_krt, ckv_s, krt_s,
                            _rope_placement(), wkt, wvp, cache_tile)

    y_prompt = _out_layer(op, xp1, g1, w_z, w_out, gf, tok_tile)
    y_sample = _out_layer(os_, xs1_flat, g1, w_z, w_out, gf, bs * t).reshape(bs, t, d)

    return (y_prompt, y_sample, st_p[None], ckv_p[None], jnp.swapaxes(krt_p, 1, 2)[None],
            st_s[None], ckv_s.reshape(1, bs, t, KV_LORA_RANK), jnp.swapaxes(krt_s, 1, 2)[None])
```

```python
import functools
import math

import jax
import jax.numpy as jnp
import numpy as np
from jax import lax
from jax.experimental import pallas as pl
from jax.experimental.pallas import tpu as pltpu

D_MODEL = 1024
CHUNK = 64
CONV_WIDTH = 3
N_HEADS = 16
QK_NOPE_DIM = 64
QK_ROPE_DIM = 32
V_HEAD_DIM = 64
QK_DIM = QK_NOPE_DIM + QK_ROPE_DIM
Q_LORA_RANK = 256
KV_LORA_RANK = 128
MLA_GATE_DIM = N_HEADS * V_HEAD_DIM
ROPE_THETA = 10000.0
EPS = 1e-6
NEG_INF = -1e30

LANES = 128
HEAD_BLOCK = LANES
N_PAIRS = N_HEADS // 2
VT_ROWS = V_HEAD_DIM + 16
ROPE_LO = QK_NOPE_DIM
ROPE_MID = ROPE_LO + QK_ROPE_DIM // 2
ROPE_HI = ROPE_LO + QK_ROPE_DIM
Q_SCALE = (QK_DIM ** -0.5) * math.log2(math.e)

V7X_VMEM_BYTES = 64 * 1024 * 1024
VMEM_LIMIT = V7X_VMEM_BYTES - 8 * 1024 * 1024
F32 = jnp.float32
BF16 = jnp.bfloat16


def _rms(x, g):
    ms = jnp.mean(x * x, axis=-1, keepdims=True)
    return x * lax.rsqrt(ms + EPS) * g


def _const_spec(shape):
    nd = len(shape)
    return pl.BlockSpec(shape, lambda *_: (0,) * nd)


def _conv_kernel(x_ref, prev_ref, g_ref, win_ref, cw_ref, wout_ref, xo_ref, st_ref, ubuf):
    j = pl.program_id(1)
    n_seq = ubuf.shape[0]
    t = x_ref.shape[1] // n_seq
    halo = CONV_WIDTH - 1
    base = 8

    @pl.when(j == 0)
    def _():
        for s in range(n_seq):
            ubuf[s, base - halo:base, :] = prev_ref[s]

    x = x_ref[0]
    hb = _rms(x, g_ref[...]).astype(BF16)

    def proj(k):
        return jnp.dot(hb, win_ref[:, k * D_MODEL:(k + 1) * D_MODEL],
                       preferred_element_type=F32)

    u = proj(1) * proj(2)
    convs, new_halos = [], []
    for s in range(n_seq):
        us = u[s * t:(s + 1) * t, :]
        ubuf[s, base:base + t, :] = us
        convs.append(ubuf[s, base - 2:base - 2 + t, :] * cw_ref[0:1, :]
                     + ubuf[s, base - 1:base - 1 + t, :] * cw_ref[1:2, :]
                     + us * cw_ref[2:3, :])
        new_halos.append(ubuf[s, base + t - halo:base + t, :])
        ubuf[s, base - halo:base, :] = new_halos[s]
    conv = convs[0] if n_seq == 1 else jnp.concatenate(convs, axis=0)
    z = proj(3)
    gated = (z * jax.nn.sigmoid(z)) * proj(0) * conv
    y = jnp.dot(gated.astype(BF16), wout_ref[...], preferred_element_type=F32)
    xo_ref[0] = x + y

    @pl.when(j == pl.num_programs(1) - 1)
    def _():
        for s in range(n_seq):
            st_ref[s] = new_halos[s]


def _conv_layer(x, prev, g, w_in, conv_w, w_out, tile):
    b, s, d = x.shape
    n = s // tile
    n_seq = prev.shape[0] // b
    assert n_seq == 1 or (n == 1 and tile % (8 * n_seq) == 0)
    return pl.pallas_call(
        _conv_kernel,
        grid=(b, n),
        in_specs=[
            pl.BlockSpec((1, tile, d), lambda i, j: (i, j, 0)),
            pl.BlockSpec((n_seq, CONV_WIDTH - 1, d), lambda i, j: (i, 0, 0)),
            _const_spec((1, d)),
            _const_spec(w_in.shape),
            _const_spec(conv_w.shape),
            _const_spec(w_out.shape),
        ],
        out_specs=[
            pl.BlockSpec((1, tile, d), lambda i, j: (i, j, 0)),
            pl.BlockSpec((n_seq, CONV_WIDTH - 1, d), lambda i, j: (i, 0, 0)),
        ],
        out_shape=[
            jax.ShapeDtypeStruct((b, s, d), F32),
            jax.ShapeDtypeStruct((b * n_seq, CONV_WIDTH - 1, d), F32),
        ],
        scratch_shapes=[pltpu.VMEM((n_seq, tile // n_seq + 8, d), F32)],
        compiler_params=pltpu.CompilerParams(
            dimension_semantics=("arbitrary", "arbitrary"),
            vmem_limit_bytes=VMEM_LIMIT),
        name="conv_layer",
    )(x, prev, g, w_in, conv_w, w_out)


def _rope_table_kernel(inv_ref, cos_ref, sin_ref, cos_row, sin_row, *, offset):
    t = cos_ref.shape[0]
    j = pl.program_id(0)
    inv = inv_ref[...]

    @pl.when(j == 0)
    def _():
        row = lax.broadcasted_iota(jnp.int32, (t, LANES), 0).astype(F32)
        cos_row[...] = jnp.cos(row * inv)
        sin_row[...] = jnp.sin(row * inv)

    start = jnp.full((8, LANES), (offset + j * t).astype(F32)) * inv
    cos_start, sin_start = jnp.cos(start)[0:1, :], jnp.sin(start)[0:1, :]
    lane = lax.broadcasted_iota(jnp.int32, (t, LANES), 1)
    cos_ref[...] = cos_start * cos_row[...] - sin_start * sin_row[...]
    s = sin_start * cos_row[...] + cos_start * sin_row[...]
    sin_ref[...] = jnp.where(lane < ROPE_MID, -s, s)


def _rope_tables(inv_lane, s, offset, tile):
    return pl.pallas_call(
        functools.partial(_rope_table_kernel, offset=offset),
        grid=(s // tile,),
        in_specs=[_const_spec((1, LANES))],
        out_specs=[pl.BlockSpec((tile, LANES), lambda i: (i, 0))] * 2,
        out_shape=[jax.ShapeDtypeStruct((s, LANES), F32)] * 2,
        scratch_shapes=[pltpu.VMEM((tile, LANES), F32)] * 2,
        compiler_params=pltpu.CompilerParams(dimension_semantics=("arbitrary",)),
        name="rope_tables",
    )(inv_lane)


def _rope(v, cos, sin_signed):
    partner = pltpu.roll(v, LANES - QK_ROPE_DIM // 2, 1)
    return v * cos + partner * sin_signed


def _expand_kv(ckv, kr_block, wk_ref, wvt_ref, k_ref, vt_ref):
    t = ckv.shape[0]
    k_in = jnp.concatenate([ckv.astype(BF16), kr_block.astype(BF16)], axis=1)
    k_all = jnp.dot(k_in, wk_ref[...], preferred_element_type=F32)
    for h in range(N_HEADS):
        k_ref[0, h] = k_all[:, h * HEAD_BLOCK:(h + 1) * HEAD_BLOCK].astype(BF16)
    vt = jnp.dot(wvt_ref[...], ckv.T.astype(BF16), preferred_element_type=F32)
    ones_row = (lax.broadcasted_iota(jnp.int32, (VT_ROWS, t), 0) == V_HEAD_DIM).astype(F32)
    for h in range(N_HEADS):
        vt_ref[0, h, 0] = (vt[h * VT_ROWS:(h + 1) * VT_ROWS, :] + ones_row).astype(BF16)


def _vt_spec(tile):
    return pl.BlockSpec((1, N_HEADS, 1, VT_ROWS, tile), lambda i, j: (i, 0, j, 0, 0))


def _mla_project_kernel(x_ref, g_ref, win_ref, qg_ref, wqb_ref, kvg_ref, wk_ref, wvt_ref,
                        cos_ref, sin_ref, q_ref, ckv_ref, kr_ref, *kv_refs):
    hb = _rms(x_ref[0], g_ref[...]).astype(BF16)
    o_kv = Q_LORA_RANK
    o_kr = o_kv + KV_LORA_RANK
    o_z = o_kr + HEAD_BLOCK
    p = jnp.dot(hb, win_ref[...], preferred_element_type=F32)

    ckv = _rms(p[:, o_kv:o_kr], kvg_ref[...])
    ckv_ref[0] = ckv

    cos = cos_ref[...]
    sin_signed = sin_ref[...]
    kr_block = _rope(p[:, o_kr:o_z], cos, sin_signed)
    kr_ref[0] = kr_block.T[ROPE_LO:ROPE_HI, :]

    qn = _rms(p[:, 0:o_kv], qg_ref[...]).astype(BF16)
    q2 = jnp.dot(qn, wqb_ref[...], preferred_element_type=F32)
    q, q_partner = q2[:, :N_HEADS * HEAD_BLOCK], q2[:, N_HEADS * HEAD_BLOCK:]
    cos_q = cos * Q_SCALE
    sin_q = sin_signed * Q_SCALE
    for h in range(N_HEADS):
        hs = slice(h * HEAD_BLOCK, (h + 1) * HEAD_BLOCK)
        q_ref[0, h] = (q[:, hs] * cos_q + q_partner[:, hs] * sin_q).astype(BF16)

    if kv_refs:
        _expand_kv(ckv, kr_block, wk_ref, wvt_ref, *kv_refs)


def _mla_project(x, g, w_in2, qg, wqb, kvg, wk, wvt, cos, sin_signed, tile, emit_kv):
    b, s, d = x.shape
    n = s // tile
    tok = lambda w: pl.BlockSpec((1, tile, w), lambda i, j: (i, j, 0))
    head = lambda nh: pl.BlockSpec((1, nh, tile, LANES), lambda i, j: (i, 0, j, 0))
    out_specs = [head(N_HEADS), tok(KV_LORA_RANK),
                 pl.BlockSpec((1, QK_ROPE_DIM, tile), lambda i, j: (i, 0, j))]
    out_shape = [
        jax.ShapeDtypeStruct((b, N_HEADS, s, HEAD_BLOCK), BF16),
        jax.ShapeDtypeStruct((b, s, KV_LORA_RANK), F32),
        jax.ShapeDtypeStruct((b, QK_ROPE_DIM, s), F32),
    ]
    if emit_kv:
        out_specs += [head(N_HEADS), _vt_spec(tile)]
        out_shape += [jax.ShapeDtypeStruct((b, N_HEADS, s, HEAD_BLOCK), BF16),
                      jax.ShapeDtypeStruct((b, N_HEADS, n, VT_ROWS, tile), BF16)]
    return pl.pallas_call(
        _mla_project_kernel,
        grid=(b, n),
        in_specs=[
            tok(d),
            _const_spec((1, d)),
            _const_spec(w_in2.shape),
            _const_spec(qg.shape),
            _const_spec(wqb.shape),
            _const_spec(kvg.shape),
            _const_spec(wk.shape),
            _const_spec(wvt.shape),
            pl.BlockSpec((tile, LANES), lambda i, j: (j, 0)),
            pl.BlockSpec((tile, LANES), lambda i, j: (j, 0)),
        ],
        out_specs=out_specs,
        out_shape=out_shape,
        compiler_params=pltpu.CompilerParams(
            dimension_semantics=("arbitrary", "arbitrary"),
            vmem_limit_bytes=VMEM_LIMIT),
        name="mla_project",
    )(x, g, w_in2, qg, wqb, kvg, wk, wvt, cos, sin_signed)


def _latent_attn_kernel(q_ref, ckv_ref, kr_ref, ckvn_ref, krn_ref, place_ref, wkt_ref, wvp_ref,
                        o_ref, qa_sc, m_sc, l_sc, acc_sc, st_sc):
    j = pl.program_id(1)
    t = q_ref.shape[2]
    cols = N_HEADS * t
    col_tile = 2 * LANES

    @pl.when(j == 0)
    def _():
        lane = lax.broadcasted_iota(jnp.int32, (t, LANES), 1)
        is_rope = (lane >= ROPE_LO) & (lane < ROPE_HI)
        for h in range(N_HEADS):
            qh = q_ref[0, h]
            q_lat = jnp.dot(qh, wkt_ref[h], preferred_element_type=F32)
            qa_sc[h * t:(h + 1) * t, 0:LANES] = q_lat.astype(BF16)
            qa_sc[h * t:(h + 1) * t, LANES:2 * LANES] = jnp.where(is_rope, qh, jnp.zeros_like(qh))
        m_sc[...] = jnp.full(m_sc.shape, NEG_INF, F32)
        l_sc[...] = jnp.zeros(l_sc.shape, F32)
        acc_sc[...] = jnp.zeros(acc_sc.shape, F32)

    def attend(ckv, kr_t):
        tk = ckv.shape[0]
        kr_block = jnp.dot(place_ref[...], kr_t.astype(BF16), preferred_element_type=F32).T
        keys = jnp.concatenate([ckv.astype(BF16), kr_block.astype(BF16)], axis=1)
        ones_rows = (lax.broadcasted_iota(jnp.int32, (16, tk), 0) == 0).astype(F32)
        vals_t = jnp.concatenate([ckv.T, ones_rows], axis=0).astype(BF16)
        n_col = cols // col_tile
        col = lambda c: slice(c * col_tile, (c + 1) * col_tile)
        scores = lambda c: lax.dot_general(keys, qa_sc[col(c), :], (((1,), (1,)), ((), ())),
                                           preferred_element_type=F32)
        col_max = []
        for c in range(n_col):
            st = scores(c)
            st_sc[c, 0:tk, :] = st
            col_max.append(jnp.max(st, axis=0, keepdims=True))
        for c in range(n_col):
            cs = col(c)
            m_prev = m_sc[:, cs]
            m_new = jnp.maximum(m_prev, col_max[c])
            alpha = jnp.exp2(m_prev - m_new)
            pt = jnp.exp2(st_sc[c, 0:tk, :] - m_new).astype(BF16)
            pv = jnp.dot(vals_t, pt, preferred_element_type=F32)
            l_sc[:, cs] = alpha * l_sc[:, cs] + pv[KV_LORA_RANK:KV_LORA_RANK + 1, :]
            m_sc[:, cs] = m_new
            acc_sc[:, cs] = acc_sc[:, cs] * alpha + pv[0:KV_LORA_RANK, :]

    attend(ckv_ref[0], kr_ref[0])

    @pl.when(j == pl.num_programs(1) - 1)
    def _():
        attend(ckvn_ref[0], krn_ref[0])
        o_lat = (acc_sc[...] / l_sc[...]).T.astype(BF16)
        for p in range(N_PAIRS):
            pair = sum(jnp.dot(o_lat[h * t:(h + 1) * t, :], wvp_ref[h],
                               preferred_element_type=F32) for h in (2 * p, 2 * p + 1))
            o_ref[0, p] = pair.astype(o_ref.dtype)


def _latent_attention(q, cache_ckv, cache_krt, ckv_new, krt_new, place, wkt, wvp, tile):
    b, past = cache_ckv.shape[0], cache_ckv.shape[1]
    nh, t = q.shape[1], q.shape[2] // b
    cols = nh * t
    assert past % tile == 0 and cols % (2 * LANES) == 0
    return pl.pallas_call(
        _latent_attn_kernel,
        grid=(b, past // tile),
        in_specs=[
            pl.BlockSpec((1, nh, t, HEAD_BLOCK), lambda i, j: (0, 0, i, 0)),
            pl.BlockSpec((1, tile, KV_LORA_RANK), lambda i, j: (i, j, 0)),
            pl.BlockSpec((1, QK_ROPE_DIM, tile), lambda i, j: (i, 0, j)),
            pl.BlockSpec((1, t, KV_LORA_RANK), lambda i, j: (0, i, 0)),
            pl.BlockSpec((1, QK_ROPE_DIM, t), lambda i, j: (i, 0, 0)),
            _const_spec(place.shape), _const_spec(wkt.shape), _const_spec(wvp.shape),
        ],
        out_specs=pl.BlockSpec((1, N_PAIRS, t, LANES), lambda i, j: (0, 0, i, 0)),
        out_shape=jax.ShapeDtypeStruct((1, N_PAIRS, b * t, LANES), BF16),
        scratch_shapes=[pltpu.VMEM((cols, 2 * LANES), BF16),
                        pltpu.VMEM((1, cols), F32),
                        pltpu.VMEM((1, cols), F32),
                        pltpu.VMEM((KV_LORA_RANK, cols), F32),
                        pltpu.VMEM((cols // (2 * LANES), tile, 2 * LANES), F32)],
        compiler_params=pltpu.CompilerParams(
            dimension_semantics=("arbitrary", "arbitrary"),
            vmem_limit_bytes=VMEM_LIMIT),
        name="latent_attention",
    )(q, cache_ckv, cache_krt, ckv_new, krt_new, place, wkt, wvp)


def _attn_kernel(q_ref, qn_ref, k_ref, vt_ref, o_ref, m_sc, l_sc, acc_sc, s_sc, mx_sc, d_sc):
    n_group, tq = q_ref.shape[1], q_ref.shape[2]
    tk = vt_ref.shape[4]
    i = pl.program_id(2)
    last_q = pl.num_programs(2) - 1

    m_sc[...] = jnp.full(m_sc.shape, NEG_INF, F32)
    l_sc[...] = jnp.zeros(l_sc.shape, F32)
    acc_sc[...] = jnp.zeros(acc_sc.shape, F32)

    def scores(qr, h, j):
        k = k_ref[0, h, pl.ds(pl.multiple_of(j * tk, tk), tk), :]
        return lax.dot_general(k, qr[0, h], (((1,), (1,)), ((), ())),
                               preferred_element_type=F32)

    def issue_scores(slot, h, j):
        st = scores(q_ref, h, j)
        s_sc[slot, h] = st
        mx_sc[slot, h] = jnp.max(st, axis=0, keepdims=True)

    def issue_next_diagonal(h):
        d_sc[h] = scores(qn_ref, h, jnp.minimum(i + 1, last_q))

    def pv_dot(vt, st, m):
        return jnp.dot(vt, jnp.exp2(st - m).astype(BF16), preferred_element_type=F32)

    def online_update(h, col_max, pv_of):
        pr, lo = h // 2, (h % 2) * V_HEAD_DIM
        m_prev = m_sc[h]
        m_new = jnp.maximum(m_prev, col_max)
        alpha = jnp.exp2(m_prev - m_new)
        pv = pv_of(m_new)
        l_sc[h] = alpha * l_sc[h] + pv[V_HEAD_DIM:V_HEAD_DIM + 1, :]
        m_sc[h] = m_new
        acc_sc[pr, lo:lo + V_HEAD_DIM, :] = (acc_sc[pr, lo:lo + V_HEAD_DIM, :] * alpha
                                             + pv[0:V_HEAD_DIM, :])

    def tile(j, slot, issue):
        for h in range(n_group):
            issue(h)
            online_update(h, mx_sc[slot, h],
                          lambda m, h=h: pv_dot(vt_ref[0, h, j], s_sc[slot, h], m))

    hk, hq = tk // 2, tq // 2
    quadrant_mask = (lax.broadcasted_iota(jnp.int32, (hk, hq), 0) // CHUNK
                     <= lax.broadcasted_iota(jnp.int32, (hk, hq), 1) // CHUNK)

    def diagonal_tile(h):
        mask = quadrant_mask
        st_aa = jnp.where(mask, d_sc[h, 0:hk, 0:hq], NEG_INF)
        st_ab = d_sc[h, 0:hk, hq:tq]
        st_bb = jnp.where(mask, d_sc[h, hk:tk, hq:tq], NEG_INF)
        col_max = jnp.concatenate(
            [jnp.max(st_aa, axis=0, keepdims=True),
             jnp.maximum(jnp.max(st_ab, axis=0, keepdims=True),
                         jnp.max(st_bb, axis=0, keepdims=True))], axis=1)
        vt_a = vt_ref[0, h, i, :, 0:hk]
        vt_b = vt_ref[0, h, i, :, hk:tk]
        online_update(h, col_max, lambda m: jnp.concatenate(
            [pv_dot(vt_a, st_aa, m[:, 0:hq]),
             pv_dot(vt_a, st_ab, m[:, hq:tq]) + pv_dot(vt_b, st_bb, m[:, hq:tq])], axis=1))

    @pl.when(i == 0)
    def _():
        for h in range(n_group):
            d_sc[h] = scores(q_ref, h, 0)

    for h in range(n_group):
        issue_scores(0, h, 0)
        diagonal_tile(h)

    def tiles_from(j0, count):
        for d in range(count):
            tile(j0 + d, d % 2, lambda h, d=d: issue_scores((d + 1) % 2, h, j0 + d + 1))

    def body4(jj, carry):
        tiles_from(4 * jj, 4)
        return carry

    def body2(jj, carry):
        tiles_from(2 * jj, 2)
        return carry

    def body1(j, carry):
        tiles_from(j, 1)
        return carry

    n_loop = jnp.maximum(i - 1, 0)
    n_quads, n_pairs = n_loop // 4, n_loop // 2
    lax.fori_loop(0, n_quads, body4, 0)
    lax.fori_loop(2 * n_quads, n_pairs, body2, 0)
    lax.fori_loop(2 * n_pairs, n_loop, body1, 0)

    def write_output():
        for pr in range(n_group // 2):
            inv_l = jnp.concatenate(
                [jnp.broadcast_to(1.0 / l_sc[2 * pr + hh], (V_HEAD_DIM, tq)) for hh in range(2)],
                axis=0)
            o_ref[0, pr] = (acc_sc[pr] * inv_l).T.astype(o_ref.dtype)

    @pl.when(i > 0)
    def _():
        tile(i - 1, (i - 1) % 2, issue_next_diagonal)
        write_output()

    @pl.when(i == 0)
    def _():
        for h in range(n_group):
            issue_next_diagonal(h)
        write_output()


def _attention(q, k, vt, group):
    b, nh, sq, _ = q.shape
    skv = k.shape[2]
    n_kv, tk = vt.shape[2], vt.shape[4]
    tq = tk
    assert n_kv * tk == skv == sq and (tk // 2) % CHUNK == 0
    gp = group // 2
    n_q = sq // tq
    return pl.pallas_call(
        _attn_kernel,
        grid=(b, nh // group, n_q),
        in_specs=[
            pl.BlockSpec((1, group, tq, HEAD_BLOCK), lambda i, g, j: (i, g, j, 0)),
            pl.BlockSpec((1, group, tq, HEAD_BLOCK),
                         lambda i, g, j: (i, g, jnp.minimum(j + 1, n_q - 1), 0)),
            pl.BlockSpec((1, group, skv, HEAD_BLOCK), lambda i, g, j: (i, g, 0, 0)),
            pl.BlockSpec((1, group, n_kv, VT_ROWS, tk), lambda i, g, j: (i, g, 0, 0, 0)),
        ],
        out_specs=pl.BlockSpec((1, gp, tq, LANES), lambda i, g, j: (i, g, j, 0)),
        out_shape=jax.ShapeDtypeStruct((b, nh // 2, sq, LANES), BF16),
        scratch_shapes=[pltpu.VMEM((group, 1, tq), F32),
                        pltpu.VMEM((group, 1, tq), F32),
                        pltpu.VMEM((gp, LANES, tq), F32),
                        pltpu.VMEM((2, group, tk, tq), F32),
                        pltpu.VMEM((2, group, 1, tq), F32),
                        pltpu.VMEM((group, tk, tq), F32)],
        compiler_params=pltpu.CompilerParams(
            dimension_semantics=("arbitrary", "arbitrary", "arbitrary"),
            vmem_limit_bytes=VMEM_LIMIT),
        name="attention",
    )(q, q, k, vt)


def _out_kernel(o_ref, x_ref, g1_ref, wz_ref, w_ref, g_ref, y_ref):
    t = x_ref.shape[1]
    parts = 2 if t >= 4 * LANES else 1
    rows = [slice(p * t // parts, (p + 1) * t // parts) for p in range(parts)]
    xs = [x_ref[0, r, :] for r in rows]
    zs = [jnp.dot(_rms(x, g1_ref[...]).astype(BF16), wz_ref[...], preferred_element_type=F32)
          for x in xs]
    for r, x, z in zip(rows, xs, zs):
        o = jnp.concatenate([o_ref[0, p, r, :] for p in range(N_PAIRS)], axis=1)
        gated = (z * jax.nn.sigmoid(z) * o.astype(F32)).astype(BF16)
        x2 = x + jnp.dot(gated, w_ref[...], preferred_element_type=F32)
        y_ref[0, r, :] = _rms(x2, g_ref[...])


def _out_layer(o, x, g1, w_z, w_out, g, tile):
    b, s, d = x.shape
    n = s // tile
    tok = lambda w: pl.BlockSpec((1, tile, w), lambda i, j: (i, j, 0))
    return pl.pallas_call(
        _out_kernel,
        grid=(b, n),
        in_specs=[pl.BlockSpec((1, N_PAIRS, tile, LANES), lambda i, j: (i, 0, j, 0)),
                  tok(d), _const_spec((1, d)), _const_spec(w_z.shape),
                  _const_spec(w_out.shape), _const_spec((1, d))],
        out_specs=tok(d),
        out_shape=jax.ShapeDtypeStruct((b, s, d), F32),
        compiler_params=pltpu.CompilerParams(
            dimension_semantics=("arbitrary", "arbitrary"),
            vmem_limit_bytes=VMEM_LIMIT),
        name="out_layer",
    )(o, x, g1, w_z, w_out, g)


def _rope_block_cols(w):
    half = QK_ROPE_DIM // 2
    fill = jnp.zeros(w.shape[:-1] + (HEAD_BLOCK - ROPE_HI - half,), w.dtype)
    return jnp.concatenate([w, w[..., :half], fill], axis=-1)


def _prep_weights(mla_w_in, mla_w_qb, mla_w_kvb):
    o_kv = Q_LORA_RANK
    o_kr = o_kv + KV_LORA_RANK
    o_z = o_kr + QK_ROPE_DIM
    d = mla_w_in.shape[0]
    kr_cols = jnp.concatenate(
        [jnp.zeros((d, ROPE_LO), F32), _rope_block_cols(mla_w_in[:, o_kr:o_z])], axis=1)
    w_in2 = jnp.concatenate([mla_w_in[:, :o_kr], kr_cols], axis=1)
    w_z = mla_w_in[:, o_z:]

    wqb = mla_w_qb.reshape(Q_LORA_RANK, N_HEADS, QK_DIM)
    half = QK_ROPE_DIM // 2
    x1, x2 = wqb[..., QK_NOPE_DIM:QK_NOPE_DIM + half], wqb[..., QK_NOPE_DIM + half:]
    lane_pad = ((0, 0), (0, 0), (0, HEAD_BLOCK - QK_DIM))
    wqb_main = jnp.pad(wqb, lane_pad)
    wqb_partner = jnp.pad(jnp.concatenate([jnp.zeros_like(wqb[..., :QK_NOPE_DIM]), x2, x1], axis=-1),
                          lane_pad)
    wqb = jnp.concatenate([wqb_main.reshape(Q_LORA_RANK, N_HEADS * HEAD_BLOCK),
                           wqb_partner.reshape(Q_LORA_RANK, N_HEADS * HEAD_BLOCK)], axis=1)

    wkvb = mla_w_kvb.reshape(KV_LORA_RANK, N_HEADS, QK_NOPE_DIM + V_HEAD_DIM)
    wk_pad = jnp.pad(wkvb[:, :, :QK_NOPE_DIM], ((0, 0), (0, 0), (0, HEAD_BLOCK - QK_NOPE_DIM)))
    place = np.zeros((HEAD_BLOCK, N_HEADS, HEAD_BLOCK), np.float32)
    lanes = np.arange(ROPE_LO, ROPE_HI)
    place[lanes, :, lanes] = 1.0
    wk = jnp.concatenate([wk_pad, place], axis=0).reshape(KV_LORA_RANK + HEAD_BLOCK,
                                                          N_HEADS * HEAD_BLOCK)
    wv = wkvb[:, :, QK_NOPE_DIM:]
    wvt = jnp.pad(jnp.transpose(wv, (1, 2, 0)), ((0, 0), (0, VT_ROWS - V_HEAD_DIM), (0, 0)))
    wvt = wvt.reshape(N_HEADS * VT_ROWS, KV_LORA_RANK)

    wkt = jnp.transpose(wk_pad, (1, 2, 0))
    wv_lo = jnp.pad(wv, ((0, 0), (0, 0), (0, LANES - V_HEAD_DIM)))
    wv_hi = jnp.pad(wv, ((0, 0), (0, 0), (LANES - V_HEAD_DIM, 0)))
    odd = (jnp.arange(N_HEADS) % 2 == 1)[None, :, None]
    wvp = jnp.transpose(jnp.where(odd, wv_hi, wv_lo), (1, 0, 2))
    return tuple(w.astype(BF16) for w in (w_in2, w_z, wqb, wk, wvt, wkt, wvp))


def _inv_freq_lanes():
    inv = 1.0 / (ROPE_THETA ** (jnp.arange(0, QK_ROPE_DIM, 2, dtype=F32) / QK_ROPE_DIM))
    z = lambda n: jnp.zeros((n,), F32)
    return jnp.concatenate([z(ROPE_LO), inv, inv, z(HEAD_BLOCK - ROPE_HI)])[None, :]


def _rope_placement():
    place = np.zeros((HEAD_BLOCK, QK_ROPE_DIM), np.float32)
    rows = np.arange(QK_ROPE_DIM)
    place[ROPE_LO + rows, rows] = 1.0
    return jnp.asarray(place, BF16)


def kernel(x_prompt, x_sample, state_conv, cache_ckv, cache_krope, norm_g, final_norm_g,
           conv_w_in, conv_w, conv_w_out, mla_w_in, mla_q_norm_g, mla_w_qb,
           mla_kv_norm_g, mla_w_kvb, mla_w_out):
    bp, s, d = x_prompt.shape
    bs, t, _ = x_sample.shape
    past = cache_ckv.shape[2]

    tok_tile = 512
    attn_tile = 512
    cache_tile = 1024
    group = 4

    g0 = norm_g[0][None, :]
    g1 = norm_g[1][None, :]
    gf = final_norm_g[None, :]
    cw_in = conv_w_in[0].astype(BF16)
    cw_out = conv_w_out[0].astype(BF16)
    w_in2, w_z, wqb, wk, wvt, wkt, wvp = _prep_weights(mla_w_in[0], mla_w_qb[0], mla_w_kvb[0])
    qg = mla_q_norm_g[0][None, :]
    kvg = mla_kv_norm_g[0][None, :]
    w_out = mla_w_out[0].astype(BF16)
    inv_lane = _inv_freq_lanes()

    zeros_prev = jnp.zeros((bp, CONV_WIDTH - 1, d), F32)
    xp1, st_p = _conv_layer(x_prompt, zeros_prev, g0, cw_in, conv_w[0], cw_out, tok_tile)
    xs1_flat, st_s = _conv_layer(x_sample.reshape(1, bs * t, d), state_conv[0], g0,
                                 cw_in, conv_w[0], cw_out, bs * t)

    cos_p, sin_p = _rope_tables(inv_lane, s, 0, 512)
    cos_s, sin_s = _rope_tables(inv_lane, t, past, t)

    qp, ckv_p, krt_p, kp, vp = _mla_project(
        xp1, g1, w_in2, qg, wqb, kvg, wk, wvt, cos_p, sin_p, attn_tile, True)
    qs, ckv_s, krt_s = _mla_project(
        xs1_flat, g1, w_in2, qg, wqb, kvg, wk, wvt,
        jnp.tile(cos_s, (bs, 1)), jnp.tile(sin_s, (bs, 1)), bs * t, False)
    krt_s = jnp.swapaxes(krt_s.reshape(QK_ROPE_DIM, bs, t), 0, 1)

    op = _attention(qp, kp, vp, group)
    cache_krt = jnp.swapaxes(cache_krope[0], 1, 2)
    os_ = _latent_attention(qs, cache_ckv[0], cache_krt, ckv_s, krt_s,
                            _rope_placement(), wkt, wvp, cache_tile)

    y_prompt = _out_layer(op, xp1, g1, w_z, w_out, gf, tok_tile)
    y_sample = _out_layer(os_, xs1_flat, g1, w_z, w_out, gf, bs * t).reshape(bs, t, d)

    return (y_prompt, y_sample, st_p[None], ckv_p[None], jnp.swapaxes(krt_p, 1, 2)[None],
            st_s[None], ckv_s.reshape(1, bs, t, KV_LORA_RANK), jnp.swapaxes(krt_s, 1, 2)[None])
```

```python
import functools
import math

import jax
import jax.numpy as jnp
import numpy as np
from jax import lax
from jax.experimental import pallas as pl
from jax.experimental.pallas import tpu as pltpu

D_MODEL = 1024
CHUNK = 64
CONV_WIDTH = 3
N_HEADS = 16
QK_NOPE_DIM = 64
QK_ROPE_DIM = 32
V_HEAD_DIM = 64
QK_DIM = QK_NOPE_DIM + QK_ROPE_DIM
Q_LORA_RANK = 256
KV_LORA_RANK = 128
MLA_GATE_DIM = N_HEADS * V_HEAD_DIM
ROPE_THETA = 10000.0
EPS = 1e-6
NEG_INF = -1e30

LANES = 128
HEAD_BLOCK = LANES
N_PAIRS = N_HEADS // 2
VT_ROWS = V_HEAD_DIM + 16
ROPE_LO = QK_NOPE_DIM
ROPE_MID = ROPE_LO + QK_ROPE_DIM // 2
ROPE_HI = ROPE_LO + QK_ROPE_DIM
Q_SCALE = (QK_DIM ** -0.5) * math.log2(math.e)

V7X_VMEM_BYTES = 64 * 1024 * 1024
VMEM_LIMIT = V7X_VMEM_BYTES - 8 * 1024 * 1024
F32 = jnp.float32
BF16 = jnp.bfloat16


def _rms(x, g):
    ms = jnp.mean(x * x, axis=-1, keepdims=True)
    return x * lax.rsqrt(ms + EPS) * g


def _const_spec(shape):
    nd = len(shape)
    return pl.BlockSpec(shape, lambda *_: (0,) * nd)


def _conv_kernel(x_ref, prev_ref, g_ref, win_ref, cw_ref, wout_ref, xo_ref, st_ref, ubuf):
    j = pl.program_id(1)
    n_seq = ubuf.shape[0]
    t = x_ref.shape[1] // n_seq
    halo = CONV_WIDTH - 1
    base = 8

    @pl.when(j == 0)
    def _():
        for s in range(n_seq):
            ubuf[s, base - halo:base, :] = prev_ref[s]

    x = x_ref[0]
    hb = _rms(x, g_ref[...]).astype(BF16)

    def proj(k):
        return jnp.dot(hb, win_ref[:, k * D_MODEL:(k + 1) * D_MODEL],
                       preferred_element_type=F32)

    u = proj(1) * proj(2)
    convs, new_halos = [], []
    for s in range(n_seq):
        us = u[s * t:(s + 1) * t, :]
        ubuf[s, base:base + t, :] = us
        convs.append(ubuf[s, base - 2:base - 2 + t, :] * cw_ref[0:1, :]
                     + ubuf[s, base - 1:base - 1 + t, :] * cw_ref[1:2, :]
                     + us * cw_ref[2:3, :])
        new_halos.append(ubuf[s, base + t - halo:base + t, :])
        ubuf[s, base - halo:base, :] = new_halos[s]
    conv = convs[0] if n_seq == 1 else jnp.concatenate(convs, axis=0)
    z = proj(3)
    gated = (z * jax.nn.sigmoid(z)) * proj(0) * conv
    y = jnp.dot(gated.astype(BF16), wout_ref[...], preferred_element_type=F32)
    xo_ref[0] = x + y

    @pl.when(j == pl.num_programs(1) - 1)
    def _():
        for s in range(n_seq):
            st_ref[s] = new_halos[s]


def _conv_layer(x, prev, g, w_in, conv_w, w_out, tile):
    b, s, d = x.shape
    n = s // tile
    n_seq = prev.shape[0] // b
    assert n_seq == 1 or (n == 1 and tile % (8 * n_seq) == 0)
    return pl.pallas_call(
        _conv_kernel,
        grid=(b, n),
        in_specs=[
            pl.BlockSpec((1, tile, d), lambda i, j: (i, j, 0)),
            pl.BlockSpec((n_seq, CONV_WIDTH - 1, d), lambda i, j: (i, 0, 0)),
            _const_spec((1, d)),
            _const_spec(w_in.shape),
            _const_spec(conv_w.shape),
            _const_spec(w_out.shape),
        ],
        out_specs=[
            pl.BlockSpec((1, tile, d), lambda i, j: (i, j, 0)),
            pl.BlockSpec((n_seq, CONV_WIDTH - 1, d), lambda i, j: (i, 0, 0)),
        ],
        out_shape=[
            jax.ShapeDtypeStruct((b, s, d), F32),
            jax.ShapeDtypeStruct((b * n_seq, CONV_WIDTH - 1, d), F32),
        ],
        scratch_shapes=[pltpu.VMEM((n_seq, tile // n_seq + 8, d), F32)],
        compiler_params=pltpu.CompilerParams(
            dimension_semantics=("arbitrary", "arbitrary"),
            vmem_limit_bytes=VMEM_LIMIT),
        name="conv_layer",
    )(x, prev, g, w_in, conv_w, w_out)


def _rope_table_kernel(inv_ref, cos_ref, sin_ref, cos_row, sin_row, *, offset):
    t = cos_ref.shape[0]
    j = pl.program_id(0)
    inv = inv_ref[...]

    @pl.when(j == 0)
    def _():
        row = lax.broadcasted_iota(jnp.int32, (t, LANES), 0).astype(F32)
        cos_row[...] = jnp.cos(row * inv)
        sin_row[...] = jnp.sin(row * inv)

    start = jnp.full((8, LANES), (offset + j * t).astype(F32)) * inv
    cos_start, sin_start = jnp.cos(start)[0:1, :], jnp.sin(start)[0:1, :]
    lane = lax.broadcasted_iota(jnp.int32, (t, LANES), 1)
    cos_ref[...] = cos_start * cos_row[...] - sin_start * sin_row[...]
    s = sin_start * cos_row[...] + cos_start * sin_row[...]
    sin_ref[...] = jnp.where(lane < ROPE_MID, -s, s)


def _rope_tables(inv_lane, s, offset, tile):
    return pl.pallas_call(
        functools.partial(_rope_table_kernel, offset=offset),
        grid=(s // tile,),
        in_specs=[_const_spec((1, LANES))],
        out_specs=[pl.BlockSpec((tile, LANES), lambda i: (i, 0))] * 2,
        out_shape=[jax.ShapeDtypeStruct((s, LANES), F32)] * 2,
        scratch_shapes=[pltpu.VMEM((tile, LANES), F32)] * 2,
        compiler_params=pltpu.CompilerParams(dimension_semantics=("arbitrary",)),
        name="rope_tables",
    )(inv_lane)


def _rope(v, cos, sin_signed):
    partner = pltpu.roll(v, LANES - QK_ROPE_DIM // 2, 1)
    return v * cos + partner * sin_signed


def _expand_kv(ckv, kr_block, wk_ref, wvt_ref, k_ref, vt_ref):
    t = ckv.shape[0]
    k_in = jnp.concatenate([ckv.astype(BF16), kr_block.astype(BF16)], axis=1)
    k_all = jnp.dot(k_in, wk_ref[...], preferred_element_type=F32)
    for h in range(N_HEADS):
        k_ref[0, h] = k_all[:, h * HEAD_BLOCK:(h + 1) * HEAD_BLOCK].astype(BF16)
    vt = jnp.dot(wvt_ref[...], ckv.T.astype(BF16), preferred_element_type=F32)
    ones_row = (lax.broadcasted_iota(jnp.int32, (VT_ROWS, t), 0) == V_HEAD_DIM).astype(F32)
    for h in range(N_HEADS):
        vt_ref[0, h, 0] = (vt[h * VT_ROWS:(h + 1) * VT_ROWS, :] + ones_row).astype(BF16)


def _vt_spec(tile):
    return pl.BlockSpec((1, N_HEADS, 1, VT_ROWS, tile), lambda i, j: (i, 0, j, 0, 0))


def _mla_project_kernel(x_ref, g_ref, win_ref, qg_ref, wqb_ref, kvg_ref, wk_ref, wvt_ref,
                        cos_ref, sin_ref, q_ref, ckv_ref, kr_ref, *kv_refs):
    hb = _rms(x_ref[0], g_ref[...]).astype(BF16)
    o_kv = Q_LORA_RANK
    o_kr = o_kv + KV_LORA_RANK
    o_z = o_kr + HEAD_BLOCK
    p = jnp.dot(hb, win_ref[...], preferred_element_type=F32)

    ckv = _rms(p[:, o_kv:o_kr], kvg_ref[...])
    ckv_ref[0] = ckv

    cos = cos_ref[...]
    sin_signed = sin_ref[...]
    kr_block = _rope(p[:, o_kr:o_z], cos, sin_signed)
    kr_ref[0] = kr_block.T[ROPE_LO:ROPE_HI, :]

    qn = _rms(p[:, 0:o_kv], qg_ref[...]).astype(BF16)
    q2 = jnp.dot(qn, wqb_ref[...], preferred_element_type=F32)
    q, q_partner = q2[:, :N_HEADS * HEAD_BLOCK], q2[:, N_HEADS * HEAD_BLOCK:]
    cos_q = cos * Q_SCALE
    sin_q = sin_signed * Q_SCALE
    for h in range(N_HEADS):
        hs = slice(h * HEAD_BLOCK, (h + 1) * HEAD_BLOCK)
        qh = q[:, hs] * cos_q + q_partner[:, hs] * sin_q
        q_ref[0, h] = (qh.T if kv_refs else qh).astype(BF16)

    if kv_refs:
        _expand_kv(ckv, kr_block, wk_ref, wvt_ref, *kv_refs)


def _mla_project(x, g, w_in2, qg, wqb, kvg, wk, wvt, cos, sin_signed, tile, emit_kv):
    b, s, d = x.shape
    n = s // tile
    tok = lambda w: pl.BlockSpec((1, tile, w), lambda i, j: (i, j, 0))
    head = lambda nh: pl.BlockSpec((1, nh, tile, LANES), lambda i, j: (i, 0, j, 0))
    q_spec = (pl.BlockSpec((1, N_HEADS, HEAD_BLOCK, tile), lambda i, j: (i, 0, 0, j))
              if emit_kv else head(N_HEADS))
    q_shape = (b, N_HEADS, HEAD_BLOCK, s) if emit_kv else (b, N_HEADS, s, HEAD_BLOCK)
    out_specs = [q_spec, tok(KV_LORA_RANK),
                 pl.BlockSpec((1, QK_ROPE_DIM, tile), lambda i, j: (i, 0, j))]
    out_shape = [
        jax.ShapeDtypeStruct(q_shape, BF16),
        jax.ShapeDtypeStruct((b, s, KV_LORA_RANK), F32),
        jax.ShapeDtypeStruct((b, QK_ROPE_DIM, s), F32),
    ]
    if emit_kv:
        out_specs += [head(N_HEADS), _vt_spec(tile)]
        out_shape += [jax.ShapeDtypeStruct((b, N_HEADS, s, HEAD_BLOCK), BF16),
                      jax.ShapeDtypeStruct((b, N_HEADS, n, VT_ROWS, tile), BF16)]
    return pl.pallas_call(
        _mla_project_kernel,
        grid=(b, n),
        in_specs=[
            tok(d),
            _const_spec((1, d)),
            _const_spec(w_in2.shape),
            _const_spec(qg.shape),
            _const_spec(wqb.shape),
            _const_spec(kvg.shape),
            _const_spec(wk.shape),
            _const_spec(wvt.shape),
            pl.BlockSpec((tile, LANES), lambda i, j: (j, 0)),
            pl.BlockSpec((tile, LANES), lambda i, j: (j, 0)),
        ],
        out_specs=out_specs,
        out_shape=out_shape,
        compiler_params=pltpu.CompilerParams(
            dimension_semantics=("arbitrary", "arbitrary"),
            vmem_limit_bytes=VMEM_LIMIT),
        name="mla_project",
    )(x, g, w_in2, qg, wqb, kvg, wk, wvt, cos, sin_signed)


def _latent_attn_kernel(q_ref, ckv_ref, kr_ref, ckvn_ref, krn_ref, place_ref, wkt_ref, wvp_ref,
                        o_ref, qa_sc, m_sc, l_sc, acc_sc, st_sc):
    j = pl.program_id(1)
    t = q_ref.shape[2]
    cols = N_HEADS * t
    col_tile = 2 * LANES

    @pl.when(j == 0)
    def _():
        lane = lax.broadcasted_iota(jnp.int32, (t, LANES), 1)
        is_rope = (lane >= ROPE_LO) & (lane < ROPE_HI)
        for h in range(N_HEADS):
            qh = q_ref[0, h]
            q_lat = jnp.dot(qh, wkt_ref[h], preferred_element_type=F32)
            qa_sc[h * t:(h + 1) * t, 0:LANES] = q_lat.astype(BF16)
            qa_sc[h * t:(h + 1) * t, LANES:2 * LANES] = jnp.where(is_rope, qh, jnp.zeros_like(qh))
        m_sc[...] = jnp.full(m_sc.shape, NEG_INF, F32)
        l_sc[...] = jnp.zeros(l_sc.shape, F32)
        acc_sc[...] = jnp.zeros(acc_sc.shape, F32)

    def attend(ckv, kr_t):
        tk = ckv.shape[0]
        kr_block = jnp.dot(place_ref[...], kr_t.astype(BF16), preferred_element_type=F32).T
        keys = jnp.concatenate([ckv.astype(BF16), kr_block.astype(BF16)], axis=1)
        ones_rows = (lax.broadcasted_iota(jnp.int32, (16, tk), 0) == 0).astype(F32)
        vals_t = jnp.concatenate([ckv.T, ones_rows], axis=0).astype(BF16)
        n_col = cols // col_tile
        col = lambda c: slice(c * col_tile, (c + 1) * col_tile)
        scores = lambda c: lax.dot_general(keys, qa_sc[col(c), :], (((1,), (1,)), ((), ())),
                                           preferred_element_type=F32)
        col_max = []
        for c in range(n_col):
            st = scores(c)
            st_sc[c, 0:tk, :] = st
            col_max.append(jnp.max(st, axis=0, keepdims=True))
        for c in range(n_col):
            cs = col(c)
            m_prev = m_sc[:, cs]
            m_new = jnp.maximum(m_prev, col_max[c])
            alpha = jnp.exp2(m_prev - m_new)
            pt = jnp.exp2(st_sc[c, 0:tk, :] - m_new).astype(BF16)
            pv = jnp.dot(vals_t, pt, preferred_element_type=F32)
            l_sc[:, cs] = alpha * l_sc[:, cs] + pv[KV_LORA_RANK:KV_LORA_RANK + 1, :]
            m_sc[:, cs] = m_new
            acc_sc[:, cs] = acc_sc[:, cs] * alpha + pv[0:KV_LORA_RANK, :]

    attend(ckv_ref[0], kr_ref[0])

    @pl.when(j == pl.num_programs(1) - 1)
    def _():
        attend(ckvn_ref[0], krn_ref[0])
        o_lat = (acc_sc[...] / l_sc[...]).T.astype(BF16)
        for p in range(N_PAIRS):
            pair = sum(jnp.dot(o_lat[h * t:(h + 1) * t, :], wvp_ref[h],
                               preferred_element_type=F32) for h in (2 * p, 2 * p + 1))
            o_ref[0, p] = pair.astype(o_ref.dtype)


def _latent_attention(q, cache_ckv, cache_krt, ckv_new, krt_new, place, wkt, wvp, tile):
    b, past = cache_ckv.shape[0], cache_ckv.shape[1]
    nh, t = q.shape[1], q.shape[2] // b
    cols = nh * t
    assert past % tile == 0 and cols % (2 * LANES) == 0
    return pl.pallas_call(
        _latent_attn_kernel,
        grid=(b, past // tile),
        in_specs=[
            pl.BlockSpec((1, nh, t, HEAD_BLOCK), lambda i, j: (0, 0, i, 0)),
            pl.BlockSpec((1, tile, KV_LORA_RANK), lambda i, j: (i, j, 0)),
            pl.BlockSpec((1, QK_ROPE_DIM, tile), lambda i, j: (i, 0, j)),
            pl.BlockSpec((1, t, KV_LORA_RANK), lambda i, j: (0, i, 0)),
            pl.BlockSpec((1, QK_ROPE_DIM, t), lambda i, j: (i, 0, 0)),
            _const_spec(place.shape), _const_spec(wkt.shape), _const_spec(wvp.shape),
        ],
        out_specs=pl.BlockSpec((1, N_PAIRS, t, LANES), lambda i, j: (0, 0, i, 0)),
        out_shape=jax.ShapeDtypeStruct((1, N_PAIRS, b * t, LANES), BF16),
        scratch_shapes=[pltpu.VMEM((cols, 2 * LANES), BF16),
                        pltpu.VMEM((1, cols), F32),
                        pltpu.VMEM((1, cols), F32),
                        pltpu.VMEM((KV_LORA_RANK, cols), F32),
                        pltpu.VMEM((cols // (2 * LANES), tile, 2 * LANES), F32)],
        compiler_params=pltpu.CompilerParams(
            dimension_semantics=("arbitrary", "arbitrary"),
            vmem_limit_bytes=VMEM_LIMIT),
        name="latent_attention",
    )(q, cache_ckv, cache_krt, ckv_new, krt_new, place, wkt, wvp)


def _attn_kernel(q_ref, qn_ref, k_ref, vt_ref, o_ref, m_sc, l_sc, acc_sc, s_sc, mx_sc, d_sc):
    n_group, tq = q_ref.shape[1], q_ref.shape[3]
    tk = vt_ref.shape[4]
    i = pl.program_id(2)
    last_q = pl.num_programs(2) - 1

    m_sc[...] = jnp.full(m_sc.shape, NEG_INF, F32)
    l_sc[...] = jnp.zeros(l_sc.shape, F32)
    acc_sc[...] = jnp.zeros(acc_sc.shape, F32)

    def scores(qr, h, j):
        k = k_ref[0, h, pl.ds(pl.multiple_of(j * tk, tk), tk), :]
        return jnp.dot(k, qr[0, h], preferred_element_type=F32)

    def issue_scores(slot, h, j):
        st = scores(q_ref, h, j)
        s_sc[slot, h] = st
        mx_sc[slot, h] = jnp.max(st, axis=0, keepdims=True)

    def issue_next_diagonal(h):
        d_sc[h] = scores(qn_ref, h, jnp.minimum(i + 1, last_q))

    def pv_dot(vt, st, m):
        return jnp.dot(vt, jnp.exp2(st - m).astype(BF16), preferred_element_type=F32)

    def online_update(h, col_max, pv_of):
        pr, lo = h // 2, (h % 2) * V_HEAD_DIM
        m_prev = m_sc[h]
        m_new = jnp.maximum(m_prev, col_max)
        alpha = jnp.exp2(m_prev - m_new)
        pv = pv_of(m_new)
        l_sc[h] = alpha * l_sc[h] + pv[V_HEAD_DIM:V_HEAD_DIM + 1, :]
        m_sc[h] = m_new
        acc_sc[pr, lo:lo + V_HEAD_DIM, :] = (acc_sc[pr, lo:lo + V_HEAD_DIM, :] * alpha
                                             + pv[0:V_HEAD_DIM, :])

    def tile(j, slot, issue):
        for h in range(n_group):
            issue(h)
            online_update(h, mx_sc[slot, h],
                          lambda m, h=h: pv_dot(vt_ref[0, h, j], s_sc[slot, h], m))

    hk, hq = tk // 2, tq // 2
    quadrant_mask = (lax.broadcasted_iota(jnp.int32, (hk, hq), 0) // CHUNK
                     <= lax.broadcasted_iota(jnp.int32, (hk, hq), 1) // CHUNK)

    def diagonal_tile(h):
        mask = quadrant_mask
        st_aa = jnp.where(mask, d_sc[h, 0:hk, 0:hq], NEG_INF)
        st_ab = d_sc[h, 0:hk, hq:tq]
        st_bb = jnp.where(mask, d_sc[h, hk:tk, hq:tq], NEG_INF)
        col_max = jnp.concatenate(
            [jnp.max(st_aa, axis=0, keepdims=True),
             jnp.maximum(jnp.max(st_ab, axis=0, keepdims=True),
                         jnp.max(st_bb, axis=0, keepdims=True))], axis=1)
        vt_a = vt_ref[0, h, i, :, 0:hk]
        vt_b = vt_ref[0, h, i, :, hk:tk]
        online_update(h, col_max, lambda m: jnp.concatenate(
            [pv_dot(vt_a, st_aa, m[:, 0:hq]),
             pv_dot(vt_a, st_ab, m[:, hq:tq]) + pv_dot(vt_b, st_bb, m[:, hq:tq])], axis=1))

    @pl.when(i == 0)
    def _():
        for h in range(n_group):
            d_sc[h] = scores(q_ref, h, 0)

    for h in range(n_group):
        issue_scores(0, h, 0)
        diagonal_tile(h)

    def tiles_from(j0, count):
        for d in range(count):
            tile(j0 + d, d % 2, lambda h, d=d: issue_scores((d + 1) % 2, h, j0 + d + 1))

    def body4(jj, carry):
        tiles_from(4 * jj, 4)
        return carry

    def body2(jj, carry):
        tiles_from(2 * jj, 2)
        return carry

    def body1(j, carry):
        tiles_from(j, 1)
        return carry

    n_loop = jnp.maximum(i - 1, 0)
    n_quads, n_pairs = n_loop // 4, n_loop // 2
    lax.fori_loop(0, n_quads, body4, 0)
    lax.fori_loop(2 * n_quads, n_pairs, body2, 0)
    lax.fori_loop(2 * n_pairs, n_loop, body1, 0)

    def write_output():
        for pr in range(n_group // 2):
            inv_l = jnp.concatenate(
                [jnp.broadcast_to(1.0 / l_sc[2 * pr + hh], (V_HEAD_DIM, tq)) for hh in range(2)],
                axis=0)
            o_ref[0, pr] = (acc_sc[pr] * inv_l).T.astype(o_ref.dtype)

    @pl.when(i > 0)
    def _():
        tile(i - 1, (i - 1) % 2, issue_next_diagonal)
        write_output()

    @pl.when(i == 0)
    def _():
        for h in range(n_group):
            issue_next_diagonal(h)
        write_output()


def _attention(q, k, vt, group):
    b, nh, _, sq = q.shape
    skv = k.shape[2]
    n_kv, tk = vt.shape[2], vt.shape[4]
    tq = tk
    assert n_kv * tk == skv == sq and (tk // 2) % CHUNK == 0
    gp = group // 2
    n_q = sq // tq
    return pl.pallas_call(
        _attn_kernel,
        grid=(b, nh // group, n_q),
        in_specs=[
            pl.BlockSpec((1, group, HEAD_BLOCK, tq), lambda i, g, j: (i, g, 0, j)),
            pl.BlockSpec((1, group, HEAD_BLOCK, tq),
                         lambda i, g, j: (i, g, 0, jnp.minimum(j + 1, n_q - 1))),
            pl.BlockSpec((1, group, skv, HEAD_BLOCK), lambda i, g, j: (i, g, 0, 0)),
            pl.BlockSpec((1, group, n_kv, VT_ROWS, tk), lambda i, g, j: (i, g, 0, 0, 0)),
        ],
        out_specs=pl.BlockSpec((1, gp, tq, LANES), lambda i, g, j: (i, g, j, 0)),
        out_shape=jax.ShapeDtypeStruct((b, nh // 2, sq, LANES), BF16),
        scratch_shapes=[pltpu.VMEM((group, 1, tq), F32),
                        pltpu.VMEM((group, 1, tq), F32),
                        pltpu.VMEM((gp, LANES, tq), F32),
                        pltpu.VMEM((2, group, tk, tq), F32),
                        pltpu.VMEM((2, group, 1, tq), F32),
                        pltpu.VMEM((group, tk, tq), F32)],
        compiler_params=pltpu.CompilerParams(
            dimension_semantics=("arbitrary", "arbitrary", "arbitrary"),
            vmem_limit_bytes=VMEM_LIMIT),
        name="attention",
    )(q, q, k, vt)


def _out_kernel(o_ref, x_ref, g1_ref, wz_ref, w_ref, g_ref, y_ref):
    t = x_ref.shape[1]
    parts = 2 if t >= 4 * LANES else 1
    rows = [slice(p * t // parts, (p + 1) * t // parts) for p in range(parts)]
    xs = [x_ref[0, r, :] for r in rows]
    zs = [jnp.dot(_rms(x, g1_ref[...]).astype(BF16), wz_ref[...], preferred_element_type=F32)
          for x in xs]
    for r, x, z in zip(rows, xs, zs):
        o = jnp.concatenate([o_ref[0, p, r, :] for p in range(N_PAIRS)], axis=1)
        gated = (z * jax.nn.sigmoid(z) * o.astype(F32)).astype(BF16)
        x2 = x + jnp.dot(gated, w_ref[...], preferred_element_type=F32)
        y_ref[0, r, :] = _rms(x2, g_ref[...])


def _out_layer(o, x, g1, w_z, w_out, g, tile):
    b, s, d = x.shape
    n = s // tile
    tok = lambda w: pl.BlockSpec((1, tile, w), lambda i, j: (i, j, 0))
    return pl.pallas_call(
        _out_kernel,
        grid=(b, n),
        in_specs=[pl.BlockSpec((1, N_PAIRS, tile, LANES), lambda i, j: (i, 0, j, 0)),
                  tok(d), _const_spec((1, d)), _const_spec(w_z.shape),
                  _const_spec(w_out.shape), _const_spec((1, d))],
        out_specs=tok(d),
        out_shape=jax.ShapeDtypeStruct((b, s, d), F32),
        compiler_params=pltpu.CompilerParams(
            dimension_semantics=("arbitrary", "arbitrary"),
            vmem_limit_bytes=VMEM_LIMIT),
        name="out_layer",
    )(o, x, g1, w_z, w_out, g)


def _rope_block_cols(w):
    half = QK_ROPE_DIM // 2
    fill = jnp.zeros(w.shape[:-1] + (HEAD_BLOCK - ROPE_HI - half,), w.dtype)
    return jnp.concatenate([w, w[..., :half], fill], axis=-1)


def _prep_weights(mla_w_in, mla_w_qb, mla_w_kvb):
    o_kv = Q_LORA_RANK
    o_kr = o_kv + KV_LORA_RANK
    o_z = o_kr + QK_ROPE_DIM
    d = mla_w_in.shape[0]
    kr_cols = jnp.concatenate(
        [jnp.zeros((d, ROPE_LO), F32), _rope_block_cols(mla_w_in[:, o_kr:o_z])], axis=1)
    w_in2 = jnp.concatenate([mla_w_in[:, :o_kr], kr_cols], axis=1)
    w_z = mla_w_in[:, o_z:]

    wqb = mla_w_qb.reshape(Q_LORA_RANK, N_HEADS, QK_DIM)
    half = QK_ROPE_DIM // 2
    x1, x2 = wqb[..., QK_NOPE_DIM:QK_NOPE_DIM + half], wqb[..., QK_NOPE_DIM + half:]
    lane_pad = ((0, 0), (0, 0), (0, HEAD_BLOCK - QK_DIM))
    wqb_main = jnp.pad(wqb, lane_pad)
    wqb_partner = jnp.pad(jnp.concatenate([jnp.zeros_like(wqb[..., :QK_NOPE_DIM]), x2, x1], axis=-1),
                          lane_pad)
    wqb = jnp.concatenate([wqb_main.reshape(Q_LORA_RANK, N_HEADS * HEAD_BLOCK),
                           wqb_partner.reshape(Q_LORA_RANK, N_HEADS * HEAD_BLOCK)], axis=1)

    wkvb = mla_w_kvb.reshape(KV_LORA_RANK, N_HEADS, QK_NOPE_DIM + V_HEAD_DIM)
    wk_pad = jnp.pad(wkvb[:, :, :QK_NOPE_DIM], ((0, 0), (0, 0), (0, HEAD_BLOCK - QK_NOPE_DIM)))
    place = np.zeros((HEAD_BLOCK, N_HEADS, HEAD_BLOCK), np.float32)
    lanes = np.arange(ROPE_LO, ROPE_HI)
    place[lanes, :, lanes] = 1.0
    wk = jnp.concatenate([wk_pad, place], axis=0).reshape(KV_LORA_RANK + HEAD_BLOCK,
                                                          N_HEADS * HEAD_BLOCK)
    wv = wkvb[:, :, QK_NOPE_DIM:]
    wvt = jnp.pad(jnp.transpose(wv, (1, 2, 0)), ((0, 0), (0, VT_ROWS - V_HEAD_DIM), (0, 0)))
    wvt = wvt.reshape(N_HEADS * VT_ROWS, KV_LORA_RANK)

    wkt = jnp.transpose(wk_pad, (1, 2, 0))
    wv_lo = jnp.pad(wv, ((0, 0), (0, 0), (0, LANES - V_HEAD_DIM)))
    wv_hi = jnp.pad(wv, ((0, 0), (0, 0), (LANES - V_HEAD_DIM, 0)))
    odd = (jnp.arange(N_HEADS) % 2 == 1)[None, :, None]
    wvp = jnp.transpose(jnp.where(odd, wv_hi, wv_lo), (1, 0, 2))
    return tuple(w.astype(BF16) for w in (w_in2, w_z, wqb, wk, wvt, wkt, wvp))


def _inv_freq_lanes():
    inv = 1.0 / (ROPE_THETA ** (jnp.arange(0, QK_ROPE_DIM, 2, dtype=F32) / QK_ROPE_DIM))
    z = lambda n: jnp.zeros((n,), F32)
    return jnp.concatenate([z(ROPE_LO), inv, inv, z(HEAD_BLOCK - ROPE_HI)])[None, :]


def _rope_placement():
    place = np.zeros((HEAD_BLOCK, QK_ROPE_DIM), np.float32)
    rows = np.arange(QK_ROPE_DIM)
    place[ROPE_LO + rows, rows] = 1.0
    return jnp.asarray(place, BF16)


def kernel(x_prompt, x_sample, state_conv, cache_ckv, cache_krope, norm_g, final_norm_g,
           conv_w_in, conv_w, conv_w_out, mla_w_in, mla_q_norm_g, mla_w_qb,
           mla_kv_norm_g, mla_w_kvb, mla_w_out):
    bp, s, d = x_prompt.shape
    bs, t, _ = x_sample.shape
    past = cache_ckv.shape[2]

    tok_tile = 512
    attn_tile = 512
    cache_tile = 1024
    group = 4

    g0 = norm_g[0][None, :]
    g1 = norm_g[1][None, :]
    gf = final_norm_g[None, :]
    cw_in = conv_w_in[0].astype(BF16)
    cw_out = conv_w_out[0].astype(BF16)
    w_in2, w_z, wqb, wk, wvt, wkt, wvp = _prep_weights(mla_w_in[0], mla_w_qb[0], mla_w_kvb[0])
    qg = mla_q_norm_g[0][None, :]
    kvg = mla_kv_norm_g[0][None, :]
    w_out = mla_w_out[0].astype(BF16)
    inv_lane = _inv_freq_lanes()

    zeros_prev = jnp.zeros((bp, CONV_WIDTH - 1, d), F32)
    xp1, st_p = _conv_layer(x_prompt, zeros_prev, g0, cw_in, conv_w[0], cw_out, tok_tile)
    xs1_flat, st_s = _conv_layer(x_sample.reshape(1, bs * t, d), state_conv[0], g0,
                                 cw_in, conv_w[0], cw_out, bs * t)

    cos_p, sin_p = _rope_tables(inv_lane, s, 0, 512)
    cos_s, sin_s = _rope_tables(inv_lane, t, past, t)

    qp, ckv_p, krt_p, kp, vp = _mla_project(
        xp1, g1, w_in2, qg, wqb, kvg, wk, wvt, cos_p, sin_p, attn_tile, True)
    qs, ckv_s, krt_s = _mla_project(
        xs1_flat, g1, w_in2, qg, wqb, kvg, wk, wvt,
        jnp.tile(cos_s, (bs, 1)), jnp.tile(sin_s, (bs, 1)), bs * t, False)
    krt_s = jnp.swapaxes(krt_s.reshape(QK_ROPE_DIM, bs, t), 0, 1)

    op = _attention(qp, kp, vp, group)
    cache_krt = jnp.swapaxes(cache_krope[0], 1, 2)
    os_ = _latent_attention(qs, cache_ckv[0], cache_krt, ckv_s, krt_s,
                            _rope_placement(), wkt, wvp, cache_tile)

    y_prompt = _out_layer(op, xp1, g1, w_z, w_out, gf, tok_tile)
    y_sample = _out_layer(os_, xs1_flat, g1, w_z, w_out, gf, bs * t).reshape(bs, t, d)

    return (y_prompt, y_sample, st_p[None], ckv_p[None], jnp.swapaxes(krt_p, 1, 2)[None],
            st_s[None], ckv_s.reshape(1, bs, t, KV_LORA_RANK), jnp.swapaxes(krt_s, 1, 2)[None])
```
